```python
import jax, jax.numpy as jnp
from jax import lax
import numpy as np

D_MODEL = 1024
BATCH = 2
SEQ = 16384
DEPTH = 2

MLA_HEADS = 16
MLA_Q_RANK = 384
MLA_KV_RANK = 256
MLA_NOPE = 64
MLA_ROPE = 32
MLA_QK = MLA_NOPE + MLA_ROPE
MLA_V = 64
ROPE_THETA = 10000.0
Q_BLOCK = 128
GLA_HEADS = 4
GLA_DK = D_MODEL // 2 // GLA_HEADS
GLA_DV = D_MODEL // GLA_HEADS
GLA_GATE_RANK = 16
GLA_TAU = 16.0
GLA_CHUNK = 64
PEER_HEADS = 8
PEER_NKEYS = 128
PEER_EXPERTS = PEER_NKEYS * PEER_NKEYS
PEER_QDIM = 128
PEER_HALF = PEER_QDIM // 2
PEER_TOPK = 16
PEER_BLOCK = 128
NORM_EPS = 1e-6
N_MLA = (DEPTH + 1) // 2
N_GLA = DEPTH // 2

kernel_name = "hybrid_mla_gla_peer_trunk"


def rms_norm(x, gain):
    xf = x.astype(jnp.float32)
    y = xf * lax.rsqrt(jnp.mean(xf * xf, axis=-1, keepdims=True) + NORM_EPS)
    return (y * gain.astype(jnp.float32)).astype(x.dtype)


def apply_rope(x, positions):
    half = x.shape[-1] // 2
    inv_freq = ROPE_THETA ** (-jnp.arange(half, dtype=jnp.float32) / half)
    ang = positions.astype(jnp.float32)[..., None] * inv_freq
    cos = jnp.cos(ang)[:, :, None, :]
    sin = jnp.sin(ang)[:, :, None, :]
    x1 = x[..., :half].astype(jnp.float32)
    x2 = x[..., half:].astype(jnp.float32)
    out = jnp.concatenate([x1 * cos - x2 * sin, x2 * cos + x1 * sin], axis=-1)
    return out.astype(x.dtype)


def causal_attention(q, k, v):
    B, S, H, Dqk = q.shape
    Dv = v.shape[-1]
    nb = S // Q_BLOCK
    scale = Dqk ** -0.5
    kh = k.transpose(0, 2, 1, 3)
    vh = v.transpose(0, 2, 1, 3)
    qb = q.reshape(B, nb, Q_BLOCK, H, Dqk).transpose(1, 0, 3, 2, 4)
    key_idx = jnp.arange(S)

    def one_block(args):
        qi, blk = args
        s = jnp.einsum('bhqd,bhkd->bhqk', qi, kh).astype(jnp.float32) * scale
        q_idx = blk * Q_BLOCK + jnp.arange(Q_BLOCK)
        mask = key_idx[None, :] <= q_idx[:, None]
        s = jnp.where(mask, s, -jnp.inf)
        p = jax.nn.softmax(s, axis=-1).astype(vh.dtype)
        return jnp.einsum('bhqk,bhkd->bhqd', p, vh)

    o = lax.map(one_block, (qb, jnp.arange(nb)))
    return o.transpose(1, 0, 3, 2, 4).reshape(B, S, H, Dv)


def mla_mixer(x, positions, w_down, g_q_lat, w_uq, g_kv_lat, w_ukv, g_qn, g_kn, w_o):
    B, S, _ = x.shape
    down = x @ w_down
    c_q = rms_norm(down[..., :MLA_Q_RANK], g_q_lat)
    c_kv = rms_norm(down[..., MLA_Q_RANK:MLA_Q_RANK + MLA_KV_RANK], g_kv_lat)
    k_rope = down[..., MLA_Q_RANK + MLA_KV_RANK:]
    q = (c_q @ w_uq).reshape(B, S, MLA_HEADS, MLA_QK)
    kv = (c_kv @ w_ukv).reshape(B, S, MLA_HEADS, MLA_NOPE + MLA_V)
    k = jnp.concatenate([kv[..., :MLA_NOPE],
                         jnp.broadcast_to(k_rope[:, :, None, :], (B, S, MLA_HEADS, MLA_ROPE))], axis=-1)
    v = kv[..., MLA_NOPE:]
    q = rms_norm(q, g_qn)
    k = rms_norm(k, g_kn)
    q = jnp.concatenate([q[..., :MLA_NOPE], apply_rope(q[..., MLA_NOPE:], positions)], axis=-1)
    k = jnp.concatenate([k[..., :MLA_NOPE], apply_rope(k[..., MLA_NOPE:], positions)], axis=-1)
    o = causal_attention(q, k, v)
    return o.reshape(B, S, MLA_HEADS * MLA_V) @ w_o


def chunked_gla(q, k, v, log_a):
    B, S, H, DK = q.shape
    DV = v.shape[-1]
    C = GLA_CHUNK
    n = S // C

    def to_chunks(t):
        return t.astype(jnp.float32).reshape(B, n, C, H, t.shape[-1]).transpose(1, 0, 3, 2, 4)

    qc, kc, vc, gc = to_chunks(q), to_chunks(k), to_chunks(v), to_chunks(log_a)
    b = jnp.cumsum(gc, axis=3)
    b_last = b[..., -1:, :]
    q_dec = qc * jnp.exp(b)
    k_inv = kc * jnp.exp(-b)
    causal = jnp.tril(jnp.ones((C, C), dtype=bool))
    attn = jnp.where(causal, jnp.einsum('nbhtk,nbhsk->nbhts', q_dec, k_inv), 0.0)
    o_intra = jnp.einsum('nbhts,nbhsv->nbhtv', attn, vc)
    kv_chunk = jnp.einsum('nbhsk,nbhsv->nbhkv', kc * jnp.exp(b_last - b), vc)
    chunk_decay = jnp.exp(b_last[..., 0, :])

    def step(state, inp):
        qd, kv, dec = inp
        o = jnp.einsum('bhtk,bhkv->bhtv', qd, state)
        return state * dec[..., None] + kv, o

    state0 = jnp.zeros((B, H, DK, DV), jnp.float32)
    _, o_inter = lax.scan(step, state0, (q_dec, kv_chunk, chunk_decay))
    o = o_intra + o_inter
    return o.transpose(1, 0, 3, 2, 4).reshape(B, S, H, DV)


def gla_mixer(x, w_in, w_g2, b_g, g_on, w_o):
    B, S, _ = x.shape
    HK = GLA_HEADS * GLA_DK
    HV = GLA_HEADS * GLA_DV
    proj = x @ w_in
    q = proj[..., :HK].reshape(B, S, GLA_HEADS, GLA_DK) * (GLA_DK ** -0.5)
    k = proj[..., HK:2 * HK].reshape(B, S, GLA_HEADS, GLA_DK)
    v = proj[..., 2 * HK:2 * HK + HV].reshape(B, S, GLA_HEADS, GLA_DV)
    r = proj[..., 2 * HK + HV:2 * HK + 2 * HV]
    g_low = proj[..., 2 * HK + 2 * HV:]
    log_a = jax.nn.log_sigmoid((g_low @ w_g2 + b_g).astype(jnp.float32)) / GLA_TAU
    log_a = log_a.reshape(B, S, GLA_HEADS, GLA_DK)
    o = chunked_gla(q, k, v, log_a)
    o = rms_norm(o, g_on).reshape(B, S, HV).astype(x.dtype)
    return (o * jax.nn.silu(r)) @ w_o


def peer_ffn(x, w_query, sub_keys, u_tab, v_tab):
    B, S, D = x.shape
    T = PEER_BLOCK
    nb = S // T
    xb = x.reshape(B, nb, T, D).transpose(1, 0, 2, 3)

    def one_block(xi):
        q = (xi @ w_query).reshape(B, T, PEER_HEADS, 2, PEER_HALF)
        s = jnp.einsum('bthcd,cnd->bthcn', q, sub_keys).astype(jnp.float32)
        s_top, i_top = lax.top_k(s, PEER_TOPK)
        cand = s_top[..., 0, :, None] + s_top[..., 1, None, :]
        cand_idx = i_top[..., 0, :, None] * PEER_NKEYS + i_top[..., 1, None, :]
        best, pos = lax.top_k(cand.reshape(B, T, PEER_HEADS, PEER_TOPK * PEER_TOPK), PEER_TOPK)
        idx = jnp.take_along_axis(cand_idx.reshape(B, T, PEER_HEADS, PEER_TOPK * PEER_TOPK), pos, axis=-1)
        gate = jax.nn.softmax(best, axis=-1)
        u = u_tab[idx]
        h = jax.nn.gelu(jnp.einsum('bthkd,btd->bthk', u, xi).astype(jnp.float32), approximate=False)
        w = (gate * h).astype(xi.dtype)
        return jnp.einsum('bthk,bthkd->btd', w, v_tab[idx])

    out = lax.map(one_block, xb)
    return out.transpose(1, 0, 2, 3).reshape(B, S, D)


def setup_inputs(seed: int = 0) -> dict:
    key = jax.random.key(seed)
    ks = jax.random.split(key, 24)

    def normal(k, shape, scale):
        return jax.random.normal(k, shape, jnp.float32) * scale

    def gain(k, shape):
        return 1.0 + 0.02 * jax.random.normal(k, shape, jnp.float32)

    D = D_MODEL
    HK = GLA_HEADS * GLA_DK
    HV = GLA_HEADS * GLA_DV
    x = jax.random.normal(ks[0], (BATCH, SEQ, D), jnp.float32)
    offsets = jax.random.randint(ks[1], (BATCH, 1), 0, 4096, dtype=jnp.int32)
    positions = (offsets + jnp.arange(SEQ, dtype=jnp.int32)[None, :]).astype(jnp.int32)
    return {
        "x": x,
        "positions": positions,
        "attn_norm_g": gain(ks[2], (DEPTH, D)),
        "ffn_norm_g": gain(ks[3], (DEPTH, D)),
        "mla_w_down": normal(ks[4], (N_MLA, D, MLA_Q_RANK + MLA_KV_RANK + MLA_ROPE), D ** -0.5),
        "mla_g_q_lat": gain(ks[5], (N_MLA, MLA_Q_RANK)),
        "mla_w_uq": normal(ks[6], (N_MLA, MLA_Q_RANK, MLA_HEADS * MLA_QK), MLA_Q_RANK ** -0.5),
        "mla_g_kv_lat": gain(ks[7], (N_MLA, MLA_KV_RANK)),
        "mla_w_ukv": normal(ks[8], (N_MLA, MLA_KV_RANK, MLA_HEADS * (MLA_NOPE + MLA_V)), MLA_KV_RANK ** -0.5),
        "mla_g_qn": gain(ks[9], (N_MLA, MLA_QK)),
        "mla_g_kn": gain(ks[10], (N_MLA, MLA_QK)),
        "mla_w_o": normal(ks[11], (N_MLA, MLA_HEADS * MLA_V, D), (MLA_HEADS * MLA_V) ** -0.5),
        "gla_w_in": normal(ks[12], (N_GLA, D, 2 * HK + 2 * HV + GLA_GATE_RANK), D ** -0.5),
        "gla_w_g2": normal(ks[13], (N_GLA, GLA_GATE_RANK, HK), GLA_GATE_RANK ** -0.5),
        "gla_b_g": normal(ks[14], (N_GLA, HK), 0.02),
        "gla_g_on": gain(ks[15], (N_GLA, GLA_DV)),
        "gla_w_o": normal(ks[16], (N_GLA, HV, D), HV ** -0.5),
        "peer_w_query": normal(ks[17], (DEPTH, D, PEER_HEADS * PEER_QDIM), D ** -0.5),
        "peer_sub_keys": normal(ks[18], (DEPTH, 2, PEER_NKEYS, PEER_HALF), PEER_HALF ** -0.5),
        "peer_u": normal(ks[19], (DEPTH, PEER_EXPERTS, D), D ** -0.5),
        "peer_v": normal(ks[20], (DEPTH, PEER_EXPERTS, D), 0.5 * PEER_HEADS ** -0.5),
    }


def reference(x, positions, attn_norm_g, ffn_norm_g, mla_w_down, mla_g_q_lat, mla_w_uq, mla_g_kv_lat,
              mla_w_ukv, mla_g_qn, mla_g_kn, mla_w_o, gla_w_in, gla_w_g2, gla_b_g, gla_g_on, gla_w_o,
              peer_w_query, peer_sub_keys, peer_u, peer_v):
    h = x
    for i in range(DEPTH):
        hn = rms_norm(h, attn_norm_g[i])
        j = i // 2
        if i % 2 == 0:
            mix = mla_mixer(hn, positions, mla_w_down[j], mla_g_q_lat[j], mla_w_uq[j], mla_g_kv_lat[j],
                            mla_w_ukv[j], mla_g_qn[j], mla_g_kn[j], mla_w_o[j])
        else:
            mix = gla_mixer(hn, gla_w_in[j], gla_w_g2[j], gla_b_g[j], gla_g_on[j], gla_w_o[j])
        h = h + mix
        h = h + peer_ffn(rms_norm(h, ffn_norm_g[i]), peer_w_query[i], peer_sub_keys[i], peer_u[i], peer_v[i])
    return h
```

```python
import functools
import math

import jax
import jax.numpy as jnp
import numpy as np
from jax import lax
from jax.experimental import pallas as pl
from jax.experimental.pallas import tpu as pltpu

F32 = jnp.float32
BF16 = jnp.bfloat16
HIGHEST = lax.Precision.HIGHEST

LANES = 128
NORM_EPS = 1e-6
VMEM_LIMIT = 56 * 1024 * 1024

MLA_HEADS = 16
MLA_Q_RANK = 384
MLA_KV_RANK = 256
MLA_NOPE = 64
MLA_ROPE = 32
MLA_QK = MLA_NOPE + MLA_ROPE
MLA_V = 64
ROPE_THETA = 10000.0
GLA_HEADS = 4
GLA_DK = 128
GLA_DV = 256
GLA_GATE_RANK = 16
GLA_TAU = 16.0
GLA_CHUNK = 64
PEER_HEADS = 8
PEER_NKEYS = 128
PEER_HALF = 64
PEER_TOPK = 16
NEG_INF = float("-inf")


def _params(sem):
    return pltpu.CompilerParams(dimension_semantics=sem, vmem_limit_bytes=VMEM_LIMIT)


def _rms(x, g, n=None):
    n = x.shape[-1] if n is None else n
    ss = jnp.sum(x * x, axis=-1, keepdims=True)
    return x * lax.rsqrt(ss * (1.0 / n) + NORM_EPS) * g


def _full(shape):
    nd = len(shape)
    return pl.BlockSpec(shape, lambda *_: (0,) * nd)


def _mla_proj_kernel(x_ref, pos_ref, gin_ref, wd_ref, gql_ref, wuq_ref, gkl_ref, wuk_ref, wuv_ref,
                     gqn_ref, gkn_ref, freq_ref, ma_ref, mb_ref, q_ref, k_ref, v_ref):
    x = x_ref[...]
    hn = _rms(x, gin_ref[...])
    down = jnp.dot(hn.astype(BF16), wd_ref[...], preferred_element_type=F32)
    c_q = _rms(down[:, :MLA_Q_RANK], gql_ref[...]).astype(BF16)
    c_kv = _rms(down[:, MLA_Q_RANK:MLA_Q_RANK + MLA_KV_RANK], gkl_ref[...]).astype(BF16)
    kr = down[:, MLA_Q_RANK + MLA_KV_RANK:]
    qf = jnp.dot(c_q, wuq_ref[...], preferred_element_type=F32)
    kf = jnp.dot(c_kv, wuk_ref[...], preferred_element_type=F32)
    v_ref[...] = jnp.dot(c_kv, wuv_ref[...], preferred_element_type=F32).astype(v_ref.dtype)

    ang = pos_ref[...] * freq_ref[...]
    cos = jnp.cos(ang)
    sin = jnp.sin(ang)
    sa = sin * ma_ref[...]
    sb = sin * mb_ref[...]
    half = MLA_ROPE // 2

    def head(t, g):
        t = _rms(t, g, MLA_QK)
        return t * cos + pltpu.roll(t, LANES - half, 1) * sa + pltpu.roll(t, half, 1) * sb

    for h in range(MLA_HEADS):
        sl = slice(h * LANES, (h + 1) * LANES)
        q_ref[:, sl] = head(qf[:, sl], gqn_ref[...]).astype(q_ref.dtype)
        k_ref[:, sl] = head(kf[:, sl] + kr, gkn_ref[...]).astype(k_ref.dtype)


def _mla_proj(x2, pos2, g_in, w, tm):
    n, d = x2.shape
    hq = MLA_HEADS * LANES
    row = lambda c: pl.BlockSpec((tm, c), lambda i: (i, 0))
    return pl.pallas_call(
        _mla_proj_kernel,
        grid=(n // tm,),
        in_specs=[row(d), row(1), _full((1, d)), _full(w["wd"].shape), _full((1, MLA_Q_RANK)),
                  _full(w["wuq"].shape), _full((1, MLA_KV_RANK)), _full(w["wuk"].shape),
                  _full(w["wuv"].shape), _full((1, LANES)), _full((1, LANES)), _full((1, LANES)),
                  _full((1, LANES)), _full((1, LANES))],
        out_specs=[row(hq), row(hq), row(MLA_HEADS * MLA_V)],
        out_shape=[jax.ShapeDtypeStruct((n, hq), BF16), jax.ShapeDtypeStruct((n, hq), BF16),
                   jax.ShapeDtypeStruct((n, MLA_HEADS * MLA_V), BF16)],
        compiler_params=_params(("parallel",)),
        name="mla_proj",
    )(x2, pos2, g_in, w["wd"], w["gql"], w["wuq"], w["gkl"], w["wuk"], w["wuv"], w["gqn"], w["gkn"],
      w["freq"], w["ma"], w["mb"])


def _attn_kernel(qi_ref, ki_ref, q_ref, k_ref, v_ref, o_ref, m_scr, l_scr, acc_scr):
    p_id = pl.program_id(2)
    qi = qi_ref[p_id]
    ki = ki_ref[p_id]
    tq = q_ref.shape[0]
    tk = k_ref.shape[0]

    @pl.when(ki == 0)
    def _():
        m_scr[...] = jnp.full(m_scr.shape, NEG_INF, F32)
        l_scr[...] = jnp.zeros(l_scr.shape, F32)
        acc_scr[...] = jnp.zeros(acc_scr.shape, F32)

    def step(masked):
        v = v_ref[...]
        for hh in range(2):
            sl = slice(hh * LANES, (hh + 1) * LANES)
            s = lax.dot_general(q_ref[:, sl], k_ref[:, sl], (((1,), (1,)), ((), ())),
                                preferred_element_type=F32)
            if masked:
                row = lax.broadcasted_iota(jnp.int32, (tq, tk), 0)
                col = lax.broadcasted_iota(jnp.int32, (tq, tk), 1)
                s = jnp.where(col <= row, s, NEG_INF)
            m_prev = m_scr[hh]
            m_new = jnp.maximum(m_prev, jnp.max(s, axis=-1, keepdims=True))
            alpha = jnp.exp(m_prev - m_new)
            p = jnp.exp(s - m_new)
            l_scr[hh] = alpha * l_scr[hh] + jnp.sum(p, axis=-1, keepdims=True)
            acc_scr[hh] = alpha * acc_scr[hh] + jnp.dot(p.astype(BF16), v, preferred_element_type=F32)
            m_scr[hh] = m_new

    @pl.when(ki < qi)
    def _():
        step(False)

    @pl.when(ki == qi)
    def _():
        step(True)
        lane = lax.broadcasted_iota(jnp.int32, (tq, LANES), 1)
        o0 = acc_scr[0] / l_scr[0]
        o1 = acc_scr[1] / l_scr[1]
        o_ref[...] = jnp.where(lane < MLA_V, o0, o1).astype(o_ref.dtype)


def _attention(q, k, v, batch, seq, tq):
    nq = seq // tq
    qi_tab = np.array([qi for qi in range(nq) for _ in range(qi + 1)], np.int32)
    ki_tab = np.array([ki for qi in range(nq) for ki in range(qi + 1)], np.int32)
    npairs = len(qi_tab)
    hp = MLA_HEADS // 2
    grid_spec = pltpu.PrefetchScalarGridSpec(
        num_scalar_prefetch=2,
        grid=(batch, hp, npairs),
        in_specs=[
            pl.BlockSpec((tq, 2 * LANES), lambda b, h, p, qt, kt: (b * nq + qt[p], h)),
            pl.BlockSpec((tq, 2 * LANES), lambda b, h, p, qt, kt: (b * nq + kt[p], h)),
            pl.BlockSpec((tq, 2 * MLA_V), lambda b, h, p, qt, kt: (b * nq + kt[p], h)),
        ],
        out_specs=pl.BlockSpec((tq, 2 * MLA_V), lambda b, h, p, qt, kt: (b * nq + qt[p], h)),
        scratch_shapes=[pltpu.VMEM((2, tq, 1), F32), pltpu.VMEM((2, tq, 1), F32),
                        pltpu.VMEM((2, tq, LANES), F32)],
    )
    return pl.pallas_call(
        _attn_kernel,
        grid_spec=grid_spec,
        out_shape=jax.ShapeDtypeStruct((batch * seq, MLA_HEADS * MLA_V), BF16),
        compiler_params=_params(("parallel", "parallel", "arbitrary")),
        name="mla_attention",
    )(jnp.asarray(qi_tab), jnp.asarray(ki_tab), q, k, v)


def _sorted_top(cur, n, scr):
    for r in range(n):
        m = jnp.max(cur, axis=0, keepdims=True)
        scr[r:r + 1, :] = m
        cur = jnp.where(cur >= m, NEG_INF, cur)


def _peer_route_kernel(res_ref, a_ref, wo_ref, g_ref, wq_ref, k0_ref, k1_ref,
                       h_ref, xt_ref, s0_ref, s1_ref, thr_ref, a_scr, b_scr):
    tr = res_ref.shape[0]
    h = res_ref[...] + jnp.dot(a_ref[...], wo_ref[...], preferred_element_type=F32)
    h_ref[...] = h
    hn = _rms(h, g_ref[...])
    xt_ref[...] = hn.T.astype(xt_ref.dtype)
    q = jnp.dot(hn, wq_ref[...], precision=HIGHEST, preferred_element_type=F32)
    nk = PEER_TOPK + 1
    rows = lax.broadcasted_iota(jnp.int32, (8, tr), 0)
    a_scr[...] = jnp.full(a_scr.shape, NEG_INF, F32)
    b_scr[...] = jnp.full(b_scr.shape, NEG_INF, F32)
    contract_last = (((1,), (1,)), ((), ()))
    for hd in range(PEER_HEADS):
        qh = q[:, hd * LANES:(hd + 1) * LANES]
        s0 = lax.dot_general(k0_ref[...], qh, contract_last, precision=HIGHEST,
                             preferred_element_type=F32)
        s1 = lax.dot_general(k1_ref[...], qh, contract_last, precision=HIGHEST,
                             preferred_element_type=F32)
        _sorted_top(s0, nk, a_scr)
        _sorted_top(s1, nk, b_scr)
        pieces = [b_scr[...] + a_scr[0:1, :]]
        r = 1
        while nk // (r + 1) >= 2:
            pieces.append(jnp.where(rows < nk // (r + 1), b_scr[0:8, :] + a_scr[r:r + 1, :], NEG_INF))
            r += 1
        assert r == 8
        pieces.append(a_scr[8:24, :] + b_scr[0:1, :])
        cand = jnp.concatenate(pieces, axis=0)
        cur = cand
        best = jnp.max(cur, axis=0, keepdims=True)
        for _ in range(PEER_TOPK - 1):
            m = jnp.max(cur, axis=0, keepdims=True)
            cur = jnp.where(cur >= m, NEG_INF, cur)
        v16 = jnp.max(cur, axis=0, keepdims=True)
        cur = jnp.where(cur >= v16, NEG_INF, cur)
        v17 = jnp.max(cur, axis=0, keepdims=True)
        tau = 0.5 * (v16 + v17)
        z = jnp.sum(jnp.where(cand >= tau, jnp.exp(cand - best), 0.0), axis=0, keepdims=True)
        shift = best + jnp.log(z)
        s0_ref[hd] = s0
        s1_ref[hd] = s1 - shift
        thr_ref[hd:hd + 1, :] = tau - shift


def _peer_route(res, a, w_o, g, wq, k0, k1, tr):
    n, d = res.shape
    da = a.shape[1]
    nh = PEER_HEADS
    return pl.pallas_call(
        _peer_route_kernel,
        grid=(n // tr,),
        in_specs=[pl.BlockSpec((tr, d), lambda i: (i, 0)), pl.BlockSpec((tr, da), lambda i: (i, 0)),
                  _full(w_o.shape), _full((1, d)), _full(wq.shape), _full(k0.shape), _full(k1.shape)],
        out_specs=[pl.BlockSpec((tr, d), lambda i: (i, 0)),
                   pl.BlockSpec((d, tr), lambda i: (0, i)),
                   pl.BlockSpec((nh, PEER_NKEYS, tr), lambda i: (0, 0, i)),
                   pl.BlockSpec((nh, PEER_NKEYS, tr), lambda i: (0, 0, i)),
                   pl.BlockSpec((nh, tr), lambda i: (0, i))],
        out_shape=[jax.ShapeDtypeStruct((n, d), F32), jax.ShapeDtypeStruct((d, n), BF16),
                   jax.ShapeDtypeStruct((nh, PEER_NKEYS, n), F32),
                   jax.ShapeDtypeStruct((nh, PEER_NKEYS, n), F32),
                   jax.ShapeDtypeStruct((nh, n), F32)],
        scratch_shapes=[pltpu.VMEM((24, tr), F32), pltpu.VMEM((24, tr), F32)],
        compiler_params=_params(("parallel",)),
        name="peer_route",
    )(res, a, w_o, g, wq, k0, k1)


def _peer_dense_kernel(xt_ref, s0_ref, s1_ref, thr_ref, u_ref, vt_ref, h_ref, o_ref, acc_scr, p_scr):
    e = pl.program_id(1)
    ne = pl.num_programs(1)
    eb, t = p_scr.shape
    rows_per_step = eb // PEER_NKEYS

    @pl.when(e == 0)
    def _():
        acc_scr[...] = jnp.zeros(acc_scr.shape, F32)

    inv_sqrt2 = 1.0 / math.sqrt(2.0)
    for ii in range(rows_per_step):
        rsl = slice(ii * PEER_NKEYS, (ii + 1) * PEER_NKEYS)
        hx = jnp.dot(u_ref[rsl, :], xt_ref[...], preferred_element_type=F32)
        for c in range(t // LANES):
            csl = slice(c * LANES, (c + 1) * LANES)
            g = jnp.zeros((PEER_NKEYS, LANES), F32)
            for hd in range(PEER_HEADS):
                sm = s1_ref[hd, :, csl] + s0_ref[hd, ii:ii + 1, csl]
                g = g + jnp.where(sm >= thr_ref[hd:hd + 1, csl], jnp.exp(sm), 0.0)
            hc = hx[:, csl]
            act = 0.5 * hc * (1.0 + lax.erf(hc * inv_sqrt2))
            p_scr[rsl, csl] = (g * act).astype(p_scr.dtype)
    acc_scr[...] += jnp.dot(vt_ref[...], p_scr[...], preferred_element_type=F32)

    @pl.when(e == ne - 1)
    def _():
        o_ref[...] = h_ref[...] + acc_scr[...].T


def _peer_dense(xt, s0, s1, thr, u, vt, h, t, eb):
    d, n = xt.shape
    ne = u.shape[0] // eb
    nh = PEER_HEADS
    return pl.pallas_call(
        _peer_dense_kernel,
        grid=(n // t, ne),
        in_specs=[pl.BlockSpec((d, t), lambda i, e: (0, i)),
                  pl.BlockSpec((nh, eb // PEER_NKEYS, t), lambda i, e: (0, e, i)),
                  pl.BlockSpec((nh, PEER_NKEYS, t), lambda i, e: (0, 0, i)),
                  pl.BlockSpec((nh, t), lambda i, e: (0, i)),
                  pl.BlockSpec((eb, d), lambda i, e: (e, 0)),
                  pl.BlockSpec((d, eb), lambda i, e: (0, e)),
                  pl.BlockSpec((t, d), lambda i, e: (i, 0))],
        out_specs=pl.BlockSpec((t, d), lambda i, e: (i, 0)),
        out_shape=jax.ShapeDtypeStruct((n, d), F32),
        scratch_shapes=[pltpu.VMEM((d, t), F32), pltpu.VMEM((eb, t), BF16)],
        compiler_params=_params(("parallel", "arbitrary")),
        name="peer_dense",
    )(xt, s0, s1, thr, u, vt, h)


def _gla_proj_kernel(x_ref, gin_ref, win_ref, wg2_ref, bg_ref, q_ref, k_ref, v_ref, r_ref, la_ref):
    hk = GLA_HEADS * GLA_DK
    hv = GLA_HEADS * GLA_DV
    hn = _rms(x_ref[...], gin_ref[...])
    proj = jnp.dot(hn.astype(BF16), win_ref[...], preferred_element_type=F32)
    q_ref[...] = proj[:, :hk] * (GLA_DK ** -0.5)
    k_ref[...] = proj[:, hk:2 * hk]
    v_ref[...] = proj[:, 2 * hk:2 * hk + hv]
    r_ref[...] = proj[:, 2 * hk + hv:2 * hk + 2 * hv]
    g_low = proj[:, 2 * hk + 2 * hv:]
    z = jnp.dot(g_low, wg2_ref[...], precision=HIGHEST, preferred_element_type=F32) + bg_ref[...]
    la_ref[...] = (jnp.minimum(z, 0.0) - jnp.log1p(jnp.exp(-jnp.abs(z)))) * (1.0 / GLA_TAU)


def _gla_proj(x2, g_in, w, tm):
    n, d = x2.shape
    hk = GLA_HEADS * GLA_DK
    hv = GLA_HEADS * GLA_DV
    row = lambda c: pl.BlockSpec((tm, c), lambda i: (i, 0))
    return pl.pallas_call(
        _gla_proj_kernel,
        grid=(n // tm,),
        in_specs=[row(d), _full((1, d)), _full(w["win"].shape), _full(w["wg2"].shape), _full((1, hk))],
        out_specs=[row(hk), row(hk), row(hv), row(hv), row(hk)],
        out_shape=[jax.ShapeDtypeStruct((n, hk), F32), jax.ShapeDtypeStruct((n, hk), F32),
                   jax.ShapeDtypeStruct((n, hv), F32), jax.ShapeDtypeStruct((n, hv), F32),
                   jax.ShapeDtypeStruct((n, hk), F32)],
        compiler_params=_params(("parallel",)),
        name="gla_proj",
    )(x2, g_in, w["win"], w["wg2"], w["bg"])


def _gla_chunk_kernel(q_ref, k_ref, v_ref, r_ref, la_ref, gon_ref, o_ref, state_scr):
    c_len = GLA_CHUNK
    n_chunks = q_ref.shape[0] // c_len

    @pl.when(pl.program_id(1) == 0)
    def _():
        state_scr[...] = jnp.zeros(state_scr.shape, F32)

    ri = lax.broadcasted_iota(jnp.int32, (c_len, c_len), 0)
    ci = lax.broadcasted_iota(jnp.int32, (c_len, c_len), 1)
    causal = ci <= ri
    tril = causal.astype(F32)
    contract_last = (((1,), (1,)), ((), ()))

    def chunk(c, carry):
        r0 = pl.multiple_of(c * c_len, c_len)
        rows = pl.ds(r0, c_len)
        for h in range(GLA_HEADS):
            ksl = slice(h * GLA_DK, (h + 1) * GLA_DK)
            vsl = slice(h * GLA_DV, (h + 1) * GLA_DV)
            qc = q_ref[rows, ksl]
            kc = k_ref[rows, ksl]
            vc = v_ref[rows, vsl].astype(BF16)
            b = jnp.dot(tril, la_ref[rows, ksl], precision=HIGHEST, preferred_element_type=F32)
            b_last = b[c_len - 1:c_len, :]
            q_dec = (qc * jnp.exp(b)).astype(BF16)
            k_inv = (kc * jnp.exp(-b)).astype(BF16)
            k_rem = kc * jnp.exp(b_last - b)
            attn = lax.dot_general(q_dec, k_inv, contract_last, preferred_element_type=F32)
            attn = jnp.where(causal, attn, 0.0).astype(BF16)
            state = state_scr[h]
            o = (jnp.dot(attn, vc, preferred_element_type=F32)
                 + jnp.dot(q_dec, state.astype(BF16), preferred_element_type=F32))
            kv = jnp.dot(k_rem.T.astype(BF16), vc, preferred_element_type=F32)
            dec_col = jnp.broadcast_to(jnp.exp(b_last), (8, GLA_DK)).T[:, 0:1]
            state_scr[h] = state * dec_col + kv
            on = _rms(o, gon_ref[...])
            rr = r_ref[rows, vsl]
            o_ref[rows, vsl] = (on * (rr * jax.nn.sigmoid(rr))).astype(o_ref.dtype)
        return carry

    lax.fori_loop(0, n_chunks, chunk, 0)


def _gla_chunk(q, k, v, r, la, g_on, batch, seq, tc):
    hk = GLA_HEADS * GLA_DK
    hv = GLA_HEADS * GLA_DV
    ns = seq // tc
    row = lambda c: pl.BlockSpec((tc, c), lambda b, s: (b * ns + s, 0))
    return pl.pallas_call(
        _gla_chunk_kernel,
        grid=(batch, ns),
        in_specs=[row(hk), row(hk), row(hv), row(hv), row(hk), _full((1, GLA_DV))],
        out_specs=row(hv),
        out_shape=jax.ShapeDtypeStruct((batch * seq, hv), BF16),
        scratch_shapes=[pltpu.VMEM((GLA_HEADS, GLA_DK, GLA_DV), F32)],
        compiler_params=_params(("parallel", "arbitrary")),
        name="gla_chunk",
    )(q, k, v, r, la, g_on)


def _prep_mla(w_down, g_q_lat, w_uq, g_kv_lat, w_ukv, g_qn, g_kn):
    d = w_down.shape[0]
    lat = MLA_Q_RANK + MLA_KV_RANK
    z = lambda c: jnp.zeros((d, c), F32)
    wd = jnp.concatenate([w_down[:, :lat], z(MLA_NOPE), w_down[:, lat:], z(LANES - MLA_QK)], axis=1)
    pad_head = LANES - MLA_QK
    wuq = jnp.pad(w_uq.reshape(MLA_Q_RANK, MLA_HEADS, MLA_QK), ((0, 0), (0, 0), (0, pad_head)))
    wukv = w_ukv.reshape(MLA_KV_RANK, MLA_HEADS, MLA_NOPE + MLA_V)
    wuk = jnp.pad(wukv[:, :, :MLA_NOPE], ((0, 0), (0, 0), (0, LANES - MLA_NOPE)))
    wuv = wukv[:, :, MLA_NOPE:]
    half = MLA_ROPE // 2
    inv_freq = ROPE_THETA ** (-jnp.arange(half, dtype=F32) / half)
    zero = lambda c: jnp.zeros((c,), F32)
    freq = jnp.concatenate([zero(MLA_NOPE), inv_freq, inv_freq, zero(pad_head)])
    ma = jnp.concatenate([zero(MLA_NOPE), -jnp.ones((half,), F32), zero(half + pad_head)])
    mb = jnp.concatenate([zero(MLA_NOPE + half), jnp.ones((half,), F32), zero(pad_head)])
    scale = MLA_QK ** -0.5
    return {
        "wd": wd.astype(BF16),
        "gql": g_q_lat.reshape(1, -1), "gkl": g_kv_lat.reshape(1, -1),
        "wuq": wuq.reshape(MLA_Q_RANK, MLA_HEADS * LANES).astype(BF16),
        "wuk": wuk.reshape(MLA_KV_RANK, MLA_HEADS * LANES).astype(BF16),
        "wuv": wuv.reshape(MLA_KV_RANK, MLA_HEADS * MLA_V).astype(BF16),
        "gqn": (jnp.pad(g_qn, (0, pad_head)) * scale).reshape(1, LANES),
        "gkn": jnp.pad(g_kn, (0, pad_head)).reshape(1, LANES),
        "freq": freq.reshape(1, LANES), "ma": ma.reshape(1, LANES), "mb": mb.reshape(1, LANES),
    }


def _prep_gla(w_in, w_g2, b_g):
    pad = LANES - GLA_GATE_RANK
    return {
        "win": jnp.pad(w_in, ((0, 0), (0, pad))).astype(BF16),
        "wg2": jnp.pad(w_g2, ((0, pad), (0, 0))),
        "bg": b_g.reshape(1, -1),
    }


def _prep_peer(w_query, sub_keys, u_tab, v_tab):
    zk = jnp.zeros((PEER_NKEYS, PEER_HALF), F32)
    return {
        "wq": w_query,
        "k0": jnp.concatenate([sub_keys[0], zk], axis=1),
        "k1": jnp.concatenate([zk, sub_keys[1]], axis=1),
        "u": u_tab.astype(BF16),
        "vt": v_tab.T.astype(BF16),
    }


def _peer_layer(res, a, w_o, g, pw, tiles):
    h, xt, s0, s1, thr = _peer_route(res, a, w_o, g.reshape(1, -1), pw["wq"], pw["k0"], pw["k1"],
                                     tiles["route"])
    return _peer_dense(xt, s0, s1, thr, pw["u"], pw["vt"], h, tiles["dense_t"], tiles["dense_e"])


def _tiles(seq):
    pick = lambda want: math.gcd(want, seq)
    return {"proj": pick(512), "attn": pick(512), "route": pick(256), "dense_t": pick(512),
            "dense_e": 8 * PEER_NKEYS, "gla": pick(512)}


def kernel(x, positions, attn_norm_g, ffn_norm_g, mla_w_down, mla_g_q_lat, mla_w_uq, mla_g_kv_lat,
           mla_w_ukv, mla_g_qn, mla_g_kn, mla_w_o, gla_w_in, gla_w_g2, gla_b_g, gla_g_on, gla_w_o,
           peer_w_query, peer_sub_keys, peer_u, peer_v):
    batch, seq, d = x.shape
    n = batch * seq
    depth = attn_norm_g.shape[0]
    tiles = _tiles(seq)
    h = x.reshape(n, d)
    pos = positions.reshape(n, 1).astype(F32)
    for i in range(depth):
        j = i // 2
        g_in = attn_norm_g[i].reshape(1, d)
        if i % 2 == 0:
            w = _prep_mla(mla_w_down[j], mla_g_q_lat[j], mla_w_uq[j], mla_g_kv_lat[j], mla_w_ukv[j],
                          mla_g_qn[j], mla_g_kn[j])
            q, k, v = _mla_proj(h, pos, g_in, w, tiles["proj"])
            a = _attention(q, k, v, batch, seq, tiles["attn"])
            w_o = mla_w_o[j].astype(BF16)
        else:
            w = _prep_gla(gla_w_in[j], gla_w_g2[j], gla_b_g[j])
            q, k, v, r, la = _gla_proj(h, g_in, w, tiles["proj"])
            a = _gla_chunk(q, k, v, r, la, gla_g_on[j].reshape(1, -1), batch, seq, tiles["gla"])
            w_o = gla_w_o[j].astype(BF16)
        pw = _prep_peer(peer_w_query[i], peer_sub_keys[i], peer_u[i], peer_v[i])
        h = _peer_layer(h, a, w_o, ffn_norm_g[i], pw, tiles)
    return h.reshape(batch, seq, d)
```

```python
import functools
import math

import jax
import jax.numpy as jnp
import numpy as np
from jax import lax
from jax.experimental import pallas as pl
from jax.experimental.pallas import tpu as pltpu

F32 = jnp.float32
BF16 = jnp.bfloat16
HIGHEST = lax.Precision.HIGHEST

LANES = 128
NORM_EPS = 1e-6
VMEM_LIMIT = 56 * 1024 * 1024

MLA_HEADS = 16
MLA_Q_RANK = 384
MLA_KV_RANK = 256
MLA_NOPE = 64
MLA_ROPE = 32
MLA_QK = MLA_NOPE + MLA_ROPE
MLA_V = 64
ROPE_THETA = 10000.0
GLA_HEADS = 4
GLA_DK = 128
GLA_DV = 256
GLA_GATE_RANK = 16
GLA_TAU = 16.0
GLA_CHUNK = 64
PEER_HEADS = 8
PEER_NKEYS = 128
PEER_HALF = 64
PEER_TOPK = 16
NEG_INF = float("-inf")
ATTN_ROW_CHUNK = 256


def _params(sem):
    return pltpu.CompilerParams(dimension_semantics=sem, vmem_limit_bytes=VMEM_LIMIT)


def _rms(x, g, n=None):
    n = x.shape[-1] if n is None else n
    ss = jnp.sum(x * x, axis=-1, keepdims=True)
    return x * lax.rsqrt(ss * (1.0 / n) + NORM_EPS) * g


def _full(shape):
    nd = len(shape)
    return pl.BlockSpec(shape, lambda *_: (0,) * nd)


def _mla_proj_kernel(x_ref, pos_ref, gin_ref, wd_ref, gql_ref, wuq_ref, gkl_ref, wuk_ref, wuv_ref,
                     gqn_ref, gkn_ref, freq_ref, ma_ref, mb_ref, vone_ref, q_ref, k_ref, v_ref):
    x = x_ref[...]
    hn = _rms(x, gin_ref[...])
    down = jnp.dot(hn.astype(BF16), wd_ref[...], preferred_element_type=F32)
    c_q = _rms(down[:, :MLA_Q_RANK], gql_ref[...]).astype(BF16)
    c_kv = _rms(down[:, MLA_Q_RANK:MLA_Q_RANK + MLA_KV_RANK], gkl_ref[...]).astype(BF16)
    kr = down[:, MLA_Q_RANK + MLA_KV_RANK:]
    qf = jnp.dot(c_q, wuq_ref[...], preferred_element_type=F32)
    kf = jnp.dot(c_kv, wuk_ref[...], preferred_element_type=F32)
    v_ref[...] = (jnp.dot(c_kv, wuv_ref[...], preferred_element_type=F32) + vone_ref[...]).astype(v_ref.dtype)

    ang = pos_ref[...] * freq_ref[...]
    cos = jnp.cos(ang)
    sin = jnp.sin(ang)
    sa = sin * ma_ref[...]
    sb = sin * mb_ref[...]
    half = MLA_ROPE // 2

    def head(t, g):
        t = _rms(t, g, MLA_QK)
        return t * cos + pltpu.roll(t, LANES - half, 1) * sa + pltpu.roll(t, half, 1) * sb

    for h in range(MLA_HEADS):
        sl = slice(h * LANES, (h + 1) * LANES)
        q_ref[:, sl] = head(qf[:, sl], gqn_ref[...]).astype(q_ref.dtype)
        k_ref[:, sl] = head(kf[:, sl] + kr, gkn_ref[...]).astype(k_ref.dtype)


def _mla_proj(x2, pos2, g_in, w, tm):
    n, d = x2.shape
    hq = MLA_HEADS * LANES
    row = lambda c: pl.BlockSpec((tm, c), lambda i: (i, 0))
    return pl.pallas_call(
        _mla_proj_kernel,
        grid=(n // tm,),
        in_specs=[row(d), row(1), _full((1, d)), _full(w["wd"].shape), _full((1, MLA_Q_RANK)),
                  _full(w["wuq"].shape), _full((1, MLA_KV_RANK)), _full(w["wuk"].shape),
                  _full(w["wuv"].shape), _full((1, LANES)), _full((1, LANES)), _full((1, LANES)),
                  _full((1, LANES)), _full((1, LANES)), _full((1, hq))],
        out_specs=[row(hq), row(hq), row(hq)],
        out_shape=[jax.ShapeDtypeStruct((n, hq), BF16)] * 3,
        compiler_params=_params(("parallel",)),
        name="mla_proj",
    )(x2, pos2, g_in, w["wd"], w["gql"], w["wuq"], w["gkl"], w["wuk"], w["wuv"], w["gqn"], w["gkn"],
      w["freq"], w["ma"], w["mb"], w["vone"])


def _attn_kernel(qi_ref, ki_ref, q_ref, k_ref, v_ref, o_ref, m_scr, acc_scr):
    p_id = pl.program_id(2)
    qi = qi_ref[p_id]
    ki = ki_ref[p_id]
    tq = q_ref.shape[0]
    tk = k_ref.shape[0]
    rq = min(ATTN_ROW_CHUNK, tq)

    @pl.when(ki == 0)
    def _():
        m_scr[...] = jnp.full(m_scr.shape, NEG_INF, F32)
        acc_scr[...] = jnp.zeros(acc_scr.shape, F32)

    def step(diagonal):
        for hh in range(2):
            sl = slice(hh * LANES, (hh + 1) * LANES)
            for r in range(tq // rq):
                rows = slice(r * rq, (r + 1) * rq)
                nk = (r + 1) * rq if diagonal else tk
                s = lax.dot_general(q_ref[rows, sl], k_ref[:nk, sl], (((1,), (1,)), ((), ())),
                                    preferred_element_type=F32)
                if diagonal:
                    row = lax.broadcasted_iota(jnp.int32, (rq, rq), 0)
                    col = lax.broadcasted_iota(jnp.int32, (rq, rq), 1)
                    tail = jnp.where(col <= row, s[:, r * rq:], NEG_INF)
                    s = tail if r == 0 else jnp.concatenate([s[:, :r * rq], tail], axis=1)
                m_prev = m_scr[hh, rows]
                m_new = jnp.maximum(m_prev, jnp.max(s, axis=-1, keepdims=True))
                alpha = jnp.exp2(m_prev - m_new)
                p = jnp.concatenate(
                    [jnp.exp2(s[:, c * LANES:(c + 1) * LANES] - m_new).astype(BF16)
                     for c in range(nk // LANES)], axis=1)
                pv = jnp.dot(p, v_ref[:nk, sl], preferred_element_type=F32)
                acc_scr[hh, rows] = alpha * acc_scr[hh, rows] + pv
                m_scr[hh, rows] = m_new

    @pl.when(ki < qi)
    def _():
        step(False)

    @pl.when(ki == qi)
    def _():
        step(True)
        lane = lax.broadcasted_iota(jnp.int32, (tq, LANES), 1)
        a0 = acc_scr[0]
        a1 = acc_scr[1]
        o0 = a0 / pltpu.roll(a0, MLA_V, 1)
        o1 = pltpu.roll(a1, MLA_V, 1) / a1
        o_ref[...] = jnp.where(lane < MLA_V, o0, o1).astype(o_ref.dtype)


def _attention(q, k, v, batch, seq, tq):
    nq = seq // tq
    qi_tab = np.array([qi for qi in range(nq) for _ in range(qi + 1)], np.int32)
    ki_tab = np.array([ki for qi in range(nq) for ki in range(qi + 1)], np.int32)
    npairs = len(qi_tab)
    hp = MLA_HEADS // 2
    blk = lambda tab: pl.BlockSpec((tq, 2 * LANES), lambda b, h, p, qt, kt: (b * nq + (qt, kt)[tab][p], h))
    grid_spec = pltpu.PrefetchScalarGridSpec(
        num_scalar_prefetch=2,
        grid=(batch, hp, npairs),
        in_specs=[blk(0), blk(1), blk(1)],
        out_specs=pl.BlockSpec((tq, 2 * MLA_V), lambda b, h, p, qt, kt: (b * nq + qt[p], h)),
        scratch_shapes=[pltpu.VMEM((2, tq, LANES), F32), pltpu.VMEM((2, tq, LANES), F32)],
    )
    return pl.pallas_call(
        _attn_kernel,
        grid_spec=grid_spec,
        out_shape=jax.ShapeDtypeStruct((batch * seq, MLA_HEADS * MLA_V), BF16),
        compiler_params=_params(("parallel", "parallel", "arbitrary")),
        name="mla_attention",
    )(jnp.asarray(qi_tab), jnp.asarray(ki_tab), q, k, v)


def _sorted_top(cur, n, scr):
    for r in range(n):
        m = jnp.max(cur, axis=0, keepdims=True)
        scr[r:r + 1, :] = m
        cur = jnp.where(cur >= m, NEG_INF, cur)


def _peer_route_kernel(res_ref, a_ref, wo_ref, g_ref, wq_ref, k0_ref, k1_ref,
                       h_ref, xt_ref, s0_ref, s1_ref, thr_ref, a_scr, b_scr):
    tr = res_ref.shape[0]
    h = res_ref[...] + jnp.dot(a_ref[...], wo_ref[...], preferred_element_type=F32)
    h_ref[...] = h
    hn = _rms(h, g_ref[...])
    xt_ref[...] = hn.T.astype(xt_ref.dtype)
    q = jnp.dot(hn, wq_ref[...], precision=HIGHEST, preferred_element_type=F32)
    nk = PEER_TOPK + 1
    rows = lax.broadcasted_iota(jnp.int32, (8, tr), 0)
    a_scr[...] = jnp.full(a_scr.shape, NEG_INF, F32)
    b_scr[...] = jnp.full(b_scr.shape, NEG_INF, F32)
    contract_last = (((1,), (1,)), ((), ()))
    for hd in range(PEER_HEADS):
        qh = q[:, hd * LANES:(hd + 1) * LANES]
        s0 = lax.dot_general(k0_ref[...], qh, contract_last, precision=HIGHEST,
                             preferred_element_type=F32)
        s1 = lax.dot_general(k1_ref[...], qh, contract_last, precision=HIGHEST,
                             preferred_element_type=F32)
        _sorted_top(s0, nk, a_scr)
        _sorted_top(s1, nk, b_scr)
        pieces = [b_scr[...] + a_scr[0:1, :]]
        r = 1
        while nk // (r + 1) >= 2:
            pieces.append(jnp.where(rows < nk // (r + 1), b_scr[0:8, :] + a_scr[r:r + 1, :], NEG_INF))
            r += 1
        assert r == 8
        pieces.append(a_scr[8:24, :] + b_scr[0:1, :])
        cand = jnp.concatenate(pieces, axis=0)
        cur = cand
        best = jnp.max(cur, axis=0, keepdims=True)
        for _ in range(PEER_TOPK - 1):
            m = jnp.max(cur, axis=0, keepdims=True)
            cur = jnp.where(cur >= m, NEG_INF, cur)
        v16 = jnp.max(cur, axis=0, keepdims=True)
        cur = jnp.where(cur >= v16, NEG_INF, cur)
        v17 = jnp.max(cur, axis=0, keepdims=True)
        tau = 0.5 * (v16 + v17)
        z = jnp.sum(jnp.where(cand >= tau, jnp.exp(cand - best), 0.0), axis=0, keepdims=True)
        shift = best + jnp.log(z)
        s0_ref[hd] = s0
        s1_ref[hd] = s1 - shift
        thr_ref[hd:hd + 1, :] = tau - shift


def _peer_route(res, a, w_o, g, wq, k0, k1, tr):
    n, d = res.shape
    da = a.shape[1]
    nh = PEER_HEADS
    return pl.pallas_call(
        _peer_route_kernel,
        grid=(n // tr,),
        in_specs=[pl.BlockSpec((tr, d), lambda i: (i, 0)), pl.BlockSpec((tr, da), lambda i: (i, 0)),
                  _full(w_o.shape), _full((1, d)), _full(wq.shape), _full(k0.shape), _full(k1.shape)],
        out_specs=[pl.BlockSpec((tr, d), lambda i: (i, 0)),
                   pl.BlockSpec((d, tr), lambda i: (0, i)),
                   pl.BlockSpec((nh, PEER_NKEYS, tr), lambda i: (0, 0, i)),
                   pl.BlockSpec((nh, PEER_NKEYS, tr), lambda i: (0, 0, i)),
                   pl.BlockSpec((nh, tr), lambda i: (0, i))],
        out_shape=[jax.ShapeDtypeStruct((n, d), F32), jax.ShapeDtypeStruct((d, n), BF16),
                   jax.ShapeDtypeStruct((nh, PEER_NKEYS, n), F32),
                   jax.ShapeDtypeStruct((nh, PEER_NKEYS, n), F32),
                   jax.ShapeDtypeStruct((nh, n), F32)],
        scratch_shapes=[pltpu.VMEM((24, tr), F32), pltpu.VMEM((24, tr), F32)],
        compiler_params=_params(("parallel",)),
        name="peer_route",
    )(res, a, w_o, g, wq, k0, k1)


def _peer_dense_kernel(xt_ref, s0_ref, s1_ref, thr_ref, u_ref, vt_ref, h_ref, o_ref, acc_scr, p_scr):
    e = pl.program_id(1)
    ne = pl.num_programs(1)
    eb, t = p_scr.shape
    rows_per_step = eb // PEER_NKEYS

    @pl.when(e == 0)
    def _():
        acc_scr[...] = jnp.zeros(acc_scr.shape, F32)

    inv_sqrt2 = 1.0 / math.sqrt(2.0)
    for ii in range(rows_per_step):
        rsl = slice(ii * PEER_NKEYS, (ii + 1) * PEER_NKEYS)
        hx = jnp.dot(u_ref[rsl, :], xt_ref[...], preferred_element_type=F32)
        for c in range(t // LANES):
            csl = slice(c * LANES, (c + 1) * LANES)
            g = jnp.zeros((PEER_NKEYS, LANES), F32)
            for hd in range(PEER_HEADS):
                sm = s1_ref[hd, :, csl] + s0_ref[hd, ii:ii + 1, csl]
                g = g + jnp.where(sm >= thr_ref[hd:hd + 1, csl], jnp.exp(sm), 0.0)
            hc = hx[:, csl]
            act = 0.5 * hc * (1.0 + lax.erf(hc * inv_sqrt2))
            p_scr[rsl, csl] = (g * act).astype(p_scr.dtype)
    acc_scr[...] += jnp.dot(vt_ref[...], p_scr[...], preferred_element_type=F32)

    @pl.when(e == ne - 1)
    def _():
        o_ref[...] = h_ref[...] + acc_scr[...].T


def _peer_dense(xt, s0, s1, thr, u, vt, h, t, eb):
    d, n = xt.shape
    ne = u.shape[0] // eb
    nh = PEER_HEADS
    return pl.pallas_call(
        _peer_dense_kernel,
        grid=(n // t, ne),
        in_specs=[pl.BlockSpec((d, t), lambda i, e: (0, i)),
                  pl.BlockSpec((nh, eb // PEER_NKEYS, t), lambda i, e: (0, e, i)),
                  pl.BlockSpec((nh, PEER_NKEYS, t), lambda i, e: (0, 0, i)),
                  pl.BlockSpec((nh, t), lambda i, e: (0, i)),
                  pl.BlockSpec((eb, d), lambda i, e: (e, 0)),
                  pl.BlockSpec((d, eb), lambda i, e: (0, e)),
                  pl.BlockSpec((t, d), lambda i, e: (i, 0))],
        out_specs=pl.BlockSpec((t, d), lambda i, e: (i, 0)),
        out_shape=jax.ShapeDtypeStruct((n, d), F32),
        scratch_shapes=[pltpu.VMEM((d, t), F32), pltpu.VMEM((eb, t), BF16)],
        compiler_params=_params(("parallel", "arbitrary")),
        name="peer_dense",
    )(xt, s0, s1, thr, u, vt, h)


def _gla_proj_kernel(x_ref, gin_ref, win_ref, wg2_ref, bg_ref, q_ref, k_ref, v_ref, r_ref, la_ref):
    hk = GLA_HEADS * GLA_DK
    hv = GLA_HEADS * GLA_DV
    hn = _rms(x_ref[...], gin_ref[...])
    proj = jnp.dot(hn.astype(BF16), win_ref[...], preferred_element_type=F32)
    q_ref[...] = proj[:, :hk] * (GLA_DK ** -0.5)
    k_ref[...] = proj[:, hk:2 * hk]
    v_ref[...] = proj[:, 2 * hk:2 * hk + hv]
    r_ref[...] = proj[:, 2 * hk + hv:2 * hk + 2 * hv]
    g_low = proj[:, 2 * hk + 2 * hv:]
    z = jnp.dot(g_low, wg2_ref[...], precision=HIGHEST, preferred_element_type=F32) + bg_ref[...]
    la_ref[...] = (jnp.minimum(z, 0.0) - jnp.log1p(jnp.exp(-jnp.abs(z)))) * (1.0 / GLA_TAU)


def _gla_proj(x2, g_in, w, tm):
    n, d = x2.shape
    hk = GLA_HEADS * GLA_DK
    hv = GLA_HEADS * GLA_DV
    row = lambda c: pl.BlockSpec((tm, c), lambda i: (i, 0))
    return pl.pallas_call(
        _gla_proj_kernel,
        grid=(n // tm,),
        in_specs=[row(d), _full((1, d)), _full(w["win"].shape), _full(w["wg2"].shape), _full((1, hk))],
        out_specs=[row(hk), row(hk), row(hv), row(hv), row(hk)],
        out_shape=[jax.ShapeDtypeStruct((n, hk), F32), jax.ShapeDtypeStruct((n, hk), F32),
                   jax.ShapeDtypeStruct((n, hv), F32), jax.ShapeDtypeStruct((n, hv), F32),
                   jax.ShapeDtypeStruct((n, hk), F32)],
        compiler_params=_params(("parallel",)),
        name="gla_proj",
    )(x2, g_in, w["win"], w["wg2"], w["bg"])


def _gla_chunk_kernel(q_ref, k_ref, v_ref, r_ref, la_ref, gon_ref, o_ref, state_scr):
    c_len = GLA_CHUNK
    n_chunks = q_ref.shape[0] // c_len

    @pl.when(pl.program_id(1) == 0)
    def _():
        state_scr[...] = jnp.zeros(state_scr.shape, F32)

    ri = lax.broadcasted_iota(jnp.int32, (c_len, c_len), 0)
    ci = lax.broadcasted_iota(jnp.int32, (c_len, c_len), 1)
    causal = ci <= ri
    tril = causal.astype(F32)
    contract_last = (((1,), (1,)), ((), ()))

    def chunk(c, carry):
        r0 = pl.multiple_of(c * c_len, c_len)
        rows = pl.ds(r0, c_len)
        for h in range(GLA_HEADS):
            ksl = slice(h * GLA_DK, (h + 1) * GLA_DK)
            vsl = slice(h * GLA_DV, (h + 1) * GLA_DV)
            qc = q_ref[rows, ksl]
            kc = k_ref[rows, ksl]
            vc = v_ref[rows, vsl].astype(BF16)
            b = jnp.dot(tril, la_ref[rows, ksl], precision=HIGHEST, preferred_element_type=F32)
            b_last = b[c_len - 1:c_len, :]
            q_dec = (qc * jnp.exp(b)).astype(BF16)
            k_inv = (kc * jnp.exp(-b)).astype(BF16)
            k_rem = kc * jnp.exp(b_last - b)
            attn = lax.dot_general(q_dec, k_inv, contract_last, preferred_element_type=F32)
            attn = jnp.where(causal, attn, 0.0).astype(BF16)
            state = state_scr[h]
            o = (jnp.dot(attn, vc, preferred_element_type=F32)
                 + jnp.dot(q_dec, state.astype(BF16), preferred_element_type=F32))
            kv = jnp.dot(k_rem.T.astype(BF16), vc, preferred_element_type=F32)
            dec_col = jnp.broadcast_to(jnp.exp(b_last), (8, GLA_DK)).T[:, 0:1]
            state_scr[h] = state * dec_col + kv
            on = _rms(o, gon_ref[...])
            rr = r_ref[rows, vsl]
            o_ref[rows, vsl] = (on * (rr * jax.nn.sigmoid(rr))).astype(o_ref.dtype)
        return carry

    lax.fori_loop(0, n_chunks, chunk, 0)


def _gla_chunk(q, k, v, r, la, g_on, batch, seq, tc):
    hk = GLA_HEADS * GLA_DK
    hv = GLA_HEADS * GLA_DV
    ns = seq // tc
    row = lambda c: pl.BlockSpec((tc, c), lambda b, s: (b * ns + s, 0))
    return pl.pallas_call(
        _gla_chunk_kernel,
        grid=(batch, ns),
        in_specs=[row(hk), row(hk), row(hv), row(hv), row(hk), _full((1, GLA_DV))],
        out_specs=row(hv),
        out_shape=jax.ShapeDtypeStruct((batch * seq, hv), BF16),
        scratch_shapes=[pltpu.VMEM((GLA_HEADS, GLA_DK, GLA_DV), F32)],
        compiler_params=_params(("parallel", "arbitrary")),
        name="gla_chunk",
    )(q, k, v, r, la, g_on)


def _prep_mla(w_down, g_q_lat, w_uq, g_kv_lat, w_ukv, g_qn, g_kn):
    d = w_down.shape[0]
    lat = MLA_Q_RANK + MLA_KV_RANK
    z = lambda c: jnp.zeros((d, c), F32)
    wd = jnp.concatenate([w_down[:, :lat], z(MLA_NOPE), w_down[:, lat:], z(LANES - MLA_QK)], axis=1)
    pad_head = LANES - MLA_QK
    wuq = jnp.pad(w_uq.reshape(MLA_Q_RANK, MLA_HEADS, MLA_QK), ((0, 0), (0, 0), (0, pad_head)))
    wukv = w_ukv.reshape(MLA_KV_RANK, MLA_HEADS, MLA_NOPE + MLA_V)
    wuk = jnp.pad(wukv[:, :, :MLA_NOPE], ((0, 0), (0, 0), (0, LANES - MLA_NOPE)))
    wuv = jnp.pad(wukv[:, :, MLA_NOPE:], ((0, 0), (0, 0), (0, LANES - MLA_V)))
    half = MLA_ROPE // 2
    inv_freq = ROPE_THETA ** (-jnp.arange(half, dtype=F32) / half)
    zero = lambda c: jnp.zeros((c,), F32)
    vone = jnp.tile(jnp.concatenate([zero(MLA_V), jnp.ones((LANES - MLA_V,), F32)]), MLA_HEADS)
    freq = jnp.concatenate([zero(MLA_NOPE), inv_freq, inv_freq, zero(pad_head)])
    ma = jnp.concatenate([zero(MLA_NOPE), -jnp.ones((half,), F32), zero(half + pad_head)])
    mb = jnp.concatenate([zero(MLA_NOPE + half), jnp.ones((half,), F32), zero(pad_head)])
    scale = MLA_QK ** -0.5 * math.log2(math.e)
    return {
        "wd": wd.astype(BF16),
        "gql": g_q_lat.reshape(1, -1), "gkl": g_kv_lat.reshape(1, -1),
        "wuq": wuq.reshape(MLA_Q_RANK, MLA_HEADS * LANES).astype(BF16),
        "wuk": wuk.reshape(MLA_KV_RANK, MLA_HEADS * LANES).astype(BF16),
        "wuv": wuv.reshape(MLA_KV_RANK, MLA_HEADS * LANES).astype(BF16),
        "vone": vone.reshape(1, MLA_HEADS * LANES),
        "gqn": (jnp.pad(g_qn, (0, pad_head)) * scale).reshape(1, LANES),
        "gkn": jnp.pad(g_kn, (0, pad_head)).reshape(1, LANES),
        "freq": freq.reshape(1, LANES), "ma": ma.reshape(1, LANES), "mb": mb.reshape(1, LANES),
    }


def _prep_gla(w_in, w_g2, b_g):
    pad = LANES - GLA_GATE_RANK
    return {
        "win": jnp.pad(w_in, ((0, 0), (0, pad))).astype(BF16),
        "wg2": jnp.pad(w_g2, ((0, pad), (0, 0))),
        "bg": b_g.reshape(1, -1),
    }


def _prep_peer(w_query, sub_keys, u_tab, v_tab):
    zk = jnp.zeros((PEER_NKEYS, PEER_HALF), F32)
    return {
        "wq": w_query,
        "k0": jnp.concatenate([sub_keys[0], zk], axis=1),
        "k1": jnp.concatenate([zk, sub_keys[1]], axis=1),
        "u": u_tab.astype(BF16),
        "vt": v_tab.T.astype(BF16),
    }


def _peer_layer(res, a, w_o, g, pw, tiles):
    h, xt, s0, s1, thr = _peer_route(res, a, w_o, g.reshape(1, -1), pw["wq"], pw["k0"], pw["k1"],
                                     tiles["route"])
    return _peer_dense(xt, s0, s1, thr, pw["u"], pw["vt"], h, tiles["dense_t"], tiles["dense_e"])


def _tiles(seq):
    pick = lambda want: math.gcd(want, seq)
    return {"proj": pick(512), "attn": pick(1024), "route": pick(256), "dense_t": pick(512),
            "dense_e": 8 * PEER_NKEYS, "gla": pick(512)}


def kernel(x, positions, attn_norm_g, ffn_norm_g, mla_w_down, mla_g_q_lat, mla_w_uq, mla_g_kv_lat,
           mla_w_ukv, mla_g_qn, mla_g_kn, mla_w_o, gla_w_in, gla_w_g2, gla_b_g, gla_g_on, gla_w_o,
           peer_w_query, peer_sub_keys, peer_u, peer_v):
    batch, seq, d = x.shape
    n = batch * seq
    depth = attn_norm_g.shape[0]
    tiles = _tiles(seq)
    h = x.reshape(n, d)
    pos = positions.reshape(n, 1).astype(F32)
    for i in range(depth):
        j = i // 2
        g_in = attn_norm_g[i].reshape(1, d)
        if i % 2 == 0:
            w = _prep_mla(mla_w_down[j], mla_g_q_lat[j], mla_w_uq[j], mla_g_kv_lat[j], mla_w_ukv[j],
                          mla_g_qn[j], mla_g_kn[j])
            q, k, v = _mla_proj(h, pos, g_in, w, tiles["proj"])
            a = _attention(q, k, v, batch, seq, tiles["attn"])
            w_o = mla_w_o[j].astype(BF16)
        else:
            w = _prep_gla(gla_w_in[j], gla_w_g2[j], gla_b_g[j])
            q, k, v, r, la = _gla_proj(h, g_in, w, tiles["proj"])
            a = _gla_chunk(q, k, v, r, la, gla_g_on[j].reshape(1, -1), batch, seq, tiles["gla"])
            w_o = gla_w_o[j].astype(BF16)
        pw = _prep_peer(peer_w_query[i], peer_sub_keys[i], peer_u[i], peer_v[i])
        h = _peer_layer(h, a, w_o, ffn_norm_g[i], pw, tiles)
    return h.reshape(batch, seq, d)
```

```python
import functools
import math

import jax
import jax.numpy as jnp
import numpy as np
from jax import lax
from jax.experimental import pallas as pl
from jax.experimental.pallas import tpu as pltpu

F32 = jnp.float32
BF16 = jnp.bfloat16
HIGHEST = lax.Precision.HIGHEST

LANES = 128
NORM_EPS = 1e-6
VMEM_LIMIT = 56 * 1024 * 1024

MLA_HEADS = 16
MLA_Q_RANK = 384
MLA_KV_RANK = 256
MLA_NOPE = 64
MLA_ROPE = 32
MLA_QK = MLA_NOPE + MLA_ROPE
MLA_V = 64
ROPE_THETA = 10000.0
GLA_HEADS = 4
GLA_DK = 128
GLA_DV = 256
GLA_GATE_RANK = 16
GLA_TAU = 16.0
GLA_CHUNK = 64
PEER_HEADS = 8
PEER_NKEYS = 128
PEER_HALF = 64
PEER_TOPK = 16
NEG_INF = float("-inf")
PEER_MXU_LANES = 256
PEER_KEY_ROWS = 16
ATTN_ROW_CHUNK = 1024


def _params(sem):
    return pltpu.CompilerParams(dimension_semantics=sem, vmem_limit_bytes=VMEM_LIMIT)


def _rms(x, g, n=None):
    n = x.shape[-1] if n is None else n
    ss = jnp.sum(x * x, axis=-1, keepdims=True)
    return x * lax.rsqrt(ss * (1.0 / n) + NORM_EPS) * g


def _full(shape):
    nd = len(shape)
    return pl.BlockSpec(shape, lambda *_: (0,) * nd)


def _mla_proj_kernel(x_ref, pos_ref, gin_ref, wd_ref, gql_ref, wuq_ref, gkl_ref, wuk_ref, wuv_ref,
                     gqn_ref, gkn_ref, freq_ref, ma_ref, mb_ref, vone_ref, q_ref, k_ref, v_ref):
    x = x_ref[...]
    hn = _rms(x, gin_ref[...])
    down = jnp.dot(hn.astype(BF16), wd_ref[...], preferred_element_type=F32)
    c_q = _rms(down[:, :MLA_Q_RANK], gql_ref[...]).astype(BF16)
    c_kv = _rms(down[:, MLA_Q_RANK:MLA_Q_RANK + MLA_KV_RANK], gkl_ref[...]).astype(BF16)
    kr = down[:, MLA_Q_RANK + MLA_KV_RANK:]
    qf = jnp.dot(c_q, wuq_ref[...], preferred_element_type=F32)
    kf = jnp.dot(c_kv, wuk_ref[...], preferred_element_type=F32)
    v_ref[...] = (jnp.dot(c_kv, wuv_ref[...], preferred_element_type=F32) + vone_ref[...]).astype(v_ref.dtype)

    ang = pos_ref[...] * freq_ref[...]
    cos = jnp.cos(ang)
    sin = jnp.sin(ang)
    sa = sin * ma_ref[...]
    sb = sin * mb_ref[...]
    half = MLA_ROPE // 2

    def head(t, g):
        t = _rms(t, g, MLA_QK)
        return t * cos + pltpu.roll(t, LANES - half, 1) * sa + pltpu.roll(t, half, 1) * sb

    for h in range(MLA_HEADS):
        sl = slice(h * LANES, (h + 1) * LANES)
        q_ref[:, sl] = head(qf[:, sl], gqn_ref[...]).astype(q_ref.dtype)
        k_ref[:, sl] = head(kf[:, sl] + kr, gkn_ref[...]).astype(k_ref.dtype)


def _mla_proj(x2, pos2, g_in, w, tm):
    n, d = x2.shape
    hq = MLA_HEADS * LANES
    row = lambda c: pl.BlockSpec((tm, c), lambda i: (i, 0))
    return pl.pallas_call(
        _mla_proj_kernel,
        grid=(n // tm,),
        in_specs=[row(d), row(1), _full((1, d)), _full(w["wd"].shape), _full((1, MLA_Q_RANK)),
                  _full(w["wuq"].shape), _full((1, MLA_KV_RANK)), _full(w["wuk"].shape),
                  _full(w["wuv"].shape), _full((1, LANES)), _full((1, LANES)), _full((1, LANES)),
                  _full((1, LANES)), _full((1, LANES)), _full((1, hq))],
        out_specs=[row(hq), row(hq), row(hq)],
        out_shape=[jax.ShapeDtypeStruct((n, hq), BF16)] * 3,
        compiler_params=_params(("parallel",)),
        name="mla_proj",
    )(x2, pos2, g_in, w["wd"], w["gql"], w["wuq"], w["gkl"], w["wuk"], w["wuv"], w["gqn"], w["gkn"],
      w["freq"], w["ma"], w["mb"], w["vone"])


def _attn_kernel(qi_ref, ki_ref, q_ref, k_ref, v_ref, o_ref, m_scr, acc_scr):
    p_id = pl.program_id(2)
    qi = qi_ref[p_id]
    ki = ki_ref[p_id]
    tq = q_ref.shape[0]
    tk = k_ref.shape[0]
    rq = min(ATTN_ROW_CHUNK, tq)

    @pl.when(ki == 0)
    def _():
        m_scr[...] = jnp.full(m_scr.shape, NEG_INF, F32)
        acc_scr[...] = jnp.zeros(acc_scr.shape, F32)

    def step(diagonal):
        for hh in range(2):
            sl = slice(hh * LANES, (hh + 1) * LANES)
            for r in range(tq // rq):
                rows = slice(r * rq, (r + 1) * rq)
                nk = (r + 1) * rq if diagonal else tk
                s = lax.dot_general(q_ref[rows, sl], k_ref[:nk, sl], (((1,), (1,)), ((), ())),
                                    preferred_element_type=F32)
                if diagonal:
                    row = lax.broadcasted_iota(jnp.int32, (rq, rq), 0)
                    col = lax.broadcasted_iota(jnp.int32, (rq, rq), 1)
                    tail = jnp.where(col <= row, s[:, r * rq:], NEG_INF)
                    s = tail if r == 0 else jnp.concatenate([s[:, :r * rq], tail], axis=1)
                m_prev = m_scr[hh, rows]
                m_new = jnp.maximum(m_prev, jnp.max(s, axis=-1, keepdims=True))
                alpha = jnp.exp2(m_prev - m_new)
                p = jnp.concatenate(
                    [jnp.exp2(s[:, c * LANES:(c + 1) * LANES] - m_new).astype(BF16)
                     for c in range(nk // LANES)], axis=1)
                pv = jnp.dot(p, v_ref[:nk, sl], preferred_element_type=F32)
                acc_scr[hh, rows] = alpha * acc_scr[hh, rows] + pv
                m_scr[hh, rows] = m_new

    @pl.when(ki < qi)
    def _():
        step(False)

    @pl.when(ki == qi)
    def _():
        step(True)
        lane = lax.broadcasted_iota(jnp.int32, (tq, LANES), 1)
        a0 = acc_scr[0]
        a1 = acc_scr[1]
        o0 = a0 / pltpu.roll(a0, MLA_V, 1)
        o1 = pltpu.roll(a1, MLA_V, 1) / a1
        o_ref[...] = jnp.where(lane < MLA_V, o0, o1).astype(o_ref.dtype)


def _attention(q, k, v, batch, seq, tq):
    nq = seq // tq
    qi_tab = np.array([qi for qi in range(nq) for _ in range(qi + 1)], np.int32)
    ki_tab = np.array([ki for qi in range(nq) for ki in range(qi + 1)], np.int32)
    npairs = len(qi_tab)
    hp = MLA_HEADS // 2
    blk = lambda tab: pl.BlockSpec((tq, 2 * LANES), lambda b, h, p, qt, kt: (b * nq + (qt, kt)[tab][p], h))
    grid_spec = pltpu.PrefetchScalarGridSpec(
        num_scalar_prefetch=2,
        grid=(batch, hp, npairs),
        in_specs=[blk(0), blk(1), blk(1)],
        out_specs=pl.BlockSpec((tq, 2 * MLA_V), lambda b, h, p, qt, kt: (b * nq + qt[p], h)),
        scratch_shapes=[pltpu.VMEM((2, tq, LANES), F32), pltpu.VMEM((2, tq, LANES), F32)],
    )
    return pl.pallas_call(
        _attn_kernel,
        grid_spec=grid_spec,
        out_shape=jax.ShapeDtypeStruct((batch * seq, MLA_HEADS * MLA_V), BF16),
        compiler_params=_params(("parallel", "parallel", "arbitrary")),
        name="mla_attention",
    )(jnp.asarray(qi_tab), jnp.asarray(ki_tab), q, k, v)


def _sorted_top(cur, n, scr):
    for r in range(n):
        m = jnp.max(cur, axis=0, keepdims=True)
        scr[r:r + 1, :] = m
        cur = jnp.where(cur >= m, NEG_INF, cur)


def _peer_route_kernel(res_ref, a_ref, wo_ref, g_ref, wq_ref, k0_ref, k1_ref,
                       h_ref, xt_ref, s0_ref, s1_ref, thr_ref, a_scr, b_scr):
    tr = res_ref.shape[0]
    h = res_ref[...] + jnp.dot(a_ref[...], wo_ref[...], preferred_element_type=F32)
    h_ref[...] = h
    hn = _rms(h, g_ref[...])
    xt_ref[...] = (hn * (1.0 / math.sqrt(2.0))).T.astype(xt_ref.dtype)
    q = jnp.dot(hn, wq_ref[...], precision=HIGHEST, preferred_element_type=F32)
    nk = PEER_TOPK + 1
    rows = lax.broadcasted_iota(jnp.int32, (8, tr), 0)
    a_scr[...] = jnp.full(a_scr.shape, NEG_INF, F32)
    b_scr[...] = jnp.full(b_scr.shape, NEG_INF, F32)
    contract_last = (((1,), (1,)), ((), ()))
    for hd in range(PEER_HEADS):
        qh = q[:, hd * LANES:(hd + 1) * LANES]
        s0 = lax.dot_general(k0_ref[...], qh, contract_last, precision=HIGHEST,
                             preferred_element_type=F32)
        s1 = lax.dot_general(k1_ref[...], qh, contract_last, precision=HIGHEST,
                             preferred_element_type=F32)
        _sorted_top(s0, nk, a_scr)
        _sorted_top(s1, nk, b_scr)
        pieces = [b_scr[...] + a_scr[0:1, :]]
        r = 1
        while nk // (r + 1) >= 2:
            pieces.append(jnp.where(rows < nk // (r + 1), b_scr[0:8, :] + a_scr[r:r + 1, :], NEG_INF))
            r += 1
        assert r == 8
        pieces.append(a_scr[8:24, :] + b_scr[0:1, :])
        cand = jnp.concatenate(pieces, axis=0)
        cur = cand
        best = jnp.max(cur, axis=0, keepdims=True)
        for _ in range(PEER_TOPK - 1):
            m = jnp.max(cur, axis=0, keepdims=True)
            cur = jnp.where(cur >= m, NEG_INF, cur)
        v16 = jnp.max(cur, axis=0, keepdims=True)
        cur = jnp.where(cur >= v16, NEG_INF, cur)
        v17 = jnp.max(cur, axis=0, keepdims=True)
        tau = 0.5 * (v16 + v17)
        z = jnp.sum(jnp.where(cand >= tau, jnp.exp(cand - best), 0.0), axis=0, keepdims=True)
        shift = best + jnp.log(z) + 0.5 * math.log(2.0)
        log2e = math.log2(math.e)
        s0_ref[hd] = s0 * log2e
        s1_ref[hd] = (s1 - shift) * log2e
        thr_ref[hd:hd + 1, :] = (tau - shift) * log2e


def _peer_route(res, a, w_o, g, wq, k0, k1, tr):
    n, d = res.shape
    da = a.shape[1]
    nh = PEER_HEADS
    return pl.pallas_call(
        _peer_route_kernel,
        grid=(n // tr,),
        in_specs=[pl.BlockSpec((tr, d), lambda i: (i, 0)), pl.BlockSpec((tr, da), lambda i: (i, 0)),
                  _full(w_o.shape), _full((1, d)), _full(wq.shape), _full(k0.shape), _full(k1.shape)],
        out_specs=[pl.BlockSpec((tr, d), lambda i: (i, 0)),
                   pl.BlockSpec((d, tr), lambda i: (0, i)),
                   pl.BlockSpec((nh, PEER_NKEYS, tr), lambda i: (0, 0, i)),
                   pl.BlockSpec((nh, PEER_NKEYS, tr), lambda i: (0, 0, i)),
                   pl.BlockSpec((nh, tr), lambda i: (0, i))],
        out_shape=[jax.ShapeDtypeStruct((n, d), F32), jax.ShapeDtypeStruct((d, n), BF16),
                   jax.ShapeDtypeStruct((nh, PEER_NKEYS, n), F32),
                   jax.ShapeDtypeStruct((nh, PEER_NKEYS, n), F32),
                   jax.ShapeDtypeStruct((nh, n), F32)],
        scratch_shapes=[pltpu.VMEM((24, tr), F32), pltpu.VMEM((24, tr), F32)],
        compiler_params=_params(("parallel",)),
        name="peer_route",
    )(res, a, w_o, g, wq, k0, k1)


def _peer_dense_kernel(xt_ref, s0_ref, s1_ref, thr_ref, u_ref, vt_ref, h_ref, o_ref, acc_scr, hx_scr,
                       p_scr):
    e = pl.program_id(1)
    ne = pl.num_programs(1)
    eb, t = p_scr.shape
    rows_per_step = eb // PEER_NKEYS

    @pl.when(e == 0)
    def _():
        acc_scr[...] = jnp.zeros(acc_scr.shape, F32)

    jb = PEER_KEY_ROWS
    mw = PEER_MXU_LANES

    def scores_matmul(k):
        ksl = slice(k * mw, (k + 1) * mw)
        hx_scr[:, ksl] = jnp.dot(u_ref[...], xt_ref[:, ksl], preferred_element_type=F32)

    def weigh(c):
        csl = slice(c * LANES, (c + 1) * LANES)
        thr = [thr_ref[hd:hd + 1, csl] for hd in range(PEER_HEADS)]
        for jg in range(PEER_NKEYS // jb):
            s1t = [s1_ref[hd, jg * jb:(jg + 1) * jb, csl] for hd in range(PEER_HEADS)]
            for ii in range(rows_per_step):
                g = jnp.zeros((jb, LANES), F32)
                for hd in range(PEER_HEADS):
                    sm = s1t[hd] + s0_ref[hd, ii:ii + 1, csl]
                    g = g + jnp.where(sm >= thr[hd], jnp.exp2(sm), 0.0)
                rsl = slice(ii * PEER_NKEYS + jg * jb, ii * PEER_NKEYS + (jg + 1) * jb)
                y = hx_scr[rsl, csl]
                p_scr[rsl, csl] = (g * (y * (1.0 + lax.erf(y)))).astype(p_scr.dtype)

    scores_matmul(0)
    for k in range(t // mw):
        if k + 1 < t // mw:
            scores_matmul(k + 1)
        for c in range(k * mw // LANES, (k + 1) * mw // LANES):
            weigh(c)
        ksl = slice(k * mw, (k + 1) * mw)
        acc_scr[:, ksl] += jnp.dot(vt_ref[...], p_scr[:, ksl], preferred_element_type=F32)


    @pl.when(e == ne - 1)
    def _():
        o_ref[...] = h_ref[...] + acc_scr[...].T


def _peer_dense(xt, s0, s1, thr, u, vt, h, t, eb):
    d, n = xt.shape
    ne = u.shape[0] // eb
    nh = PEER_HEADS
    return pl.pallas_call(
        _peer_dense_kernel,
        grid=(n // t, ne),
        in_specs=[pl.BlockSpec((d, t), lambda i, e: (0, i)),
                  pl.BlockSpec((nh, eb // PEER_NKEYS, t), lambda i, e: (0, e, i)),
                  pl.BlockSpec((nh, PEER_NKEYS, t), lambda i, e: (0, 0, i)),
                  pl.BlockSpec((nh, t), lambda i, e: (0, i)),
                  pl.BlockSpec((eb, d), lambda i, e: (e, 0)),
                  pl.BlockSpec((d, eb), lambda i, e: (0, e)),
                  pl.BlockSpec((t, d), lambda i, e: (i, 0), pipeline_mode=pl.Buffered(1))],
        out_specs=pl.BlockSpec((t, d), lambda i, e: (i, 0)),
        out_shape=jax.ShapeDtypeStruct((n, d), F32),
        scratch_shapes=[pltpu.VMEM((d, t), F32), pltpu.VMEM((eb, t), F32), pltpu.VMEM((eb, t), BF16)],
        compiler_params=_params(("parallel", "arbitrary")),
        name="peer_dense",
    )(xt, s0, s1, thr, u, vt, h)


def _gla_proj_kernel(x_ref, gin_ref, win_ref, wg2_ref, bg_ref, q_ref, k_ref, v_ref, r_ref, la_ref):
    hk = GLA_HEADS * GLA_DK
    hv = GLA_HEADS * GLA_DV
    hn = _rms(x_ref[...], gin_ref[...])
    proj = jnp.dot(hn.astype(BF16), win_ref[...], preferred_element_type=F32)
    q_ref[...] = proj[:, :hk] * (GLA_DK ** -0.5)
    k_ref[...] = proj[:, hk:2 * hk]
    v_ref[...] = proj[:, 2 * hk:2 * hk + hv]
    r_ref[...] = proj[:, 2 * hk + hv:2 * hk + 2 * hv]
    g_low = proj[:, 2 * hk + 2 * hv:]
    z = jnp.dot(g_low, wg2_ref[...], precision=HIGHEST, preferred_element_type=F32) + bg_ref[...]
    la_ref[...] = (jnp.minimum(z, 0.0) - jnp.log1p(jnp.exp(-jnp.abs(z)))) * (1.0 / GLA_TAU)


def _gla_proj(x2, g_in, w, tm):
    n, d = x2.shape
    hk = GLA_HEADS * GLA_DK
    hv = GLA_HEADS * GLA_DV
    row = lambda c: pl.BlockSpec((tm, c), lambda i: (i, 0))
    return pl.pallas_call(
        _gla_proj_kernel,
        grid=(n // tm,),
        in_specs=[row(d), _full((1, d)), _full(w["win"].shape), _full(w["wg2"].shape), _full((1, hk))],
        out_specs=[row(hk), row(hk), row(hv), row(hv), row(hk)],
        out_shape=[jax.ShapeDtypeStruct((n, hk), F32), jax.ShapeDtypeStruct((n, hk), F32),
                   jax.ShapeDtypeStruct((n, hv), F32), jax.ShapeDtypeStruct((n, hv), F32),
                   jax.ShapeDtypeStruct((n, hk), F32)],
        compiler_params=_params(("parallel",)),
        name="gla_proj",
    )(x2, g_in, w["win"], w["wg2"], w["bg"])


def _gla_chunk_kernel(q_ref, k_ref, v_ref, r_ref, la_ref, gon_ref, o_ref, state_scr):
    c_len = GLA_CHUNK
    n_chunks = q_ref.shape[0] // c_len

    @pl.when(pl.program_id(1) == 0)
    def _():
        state_scr[...] = jnp.zeros(state_scr.shape, F32)

    ri = lax.broadcasted_iota(jnp.int32, (c_len, c_len), 0)
    ci = lax.broadcasted_iota(jnp.int32, (c_len, c_len), 1)
    causal = ci <= ri
    tril = causal.astype(F32)
    contract_last = (((1,), (1,)), ((), ()))

    def chunk(c, carry):
        r0 = pl.multiple_of(c * c_len, c_len)
        rows = pl.ds(r0, c_len)
        for h in range(GLA_HEADS):
            ksl = slice(h * GLA_DK, (h + 1) * GLA_DK)
            vsl = slice(h * GLA_DV, (h + 1) * GLA_DV)
            qc = q_ref[rows, ksl]
            kc = k_ref[rows, ksl]
            vc = v_ref[rows, vsl].astype(BF16)
            b = jnp.dot(tril, la_ref[rows, ksl], precision=HIGHEST, preferred_element_type=F32)
            b_last = b[c_len - 1:c_len, :]
            q_dec = (qc * jnp.exp(b)).astype(BF16)
            k_inv = (kc * jnp.exp(-b)).astype(BF16)
            k_rem = kc * jnp.exp(b_last - b)
            attn = lax.dot_general(q_dec, k_inv, contract_last, preferred_element_type=F32)
            attn = jnp.where(causal, attn, 0.0).astype(BF16)
            state = state_scr[h]
            o = (jnp.dot(attn, vc, preferred_element_type=F32)
                 + jnp.dot(q_dec, state.astype(BF16), preferred_element_type=F32))
            kv = jnp.dot(k_rem.T.astype(BF16), vc, preferred_element_type=F32)
            dec_col = jnp.broadcast_to(jnp.exp(b_last), (8, GLA_DK)).T[:, 0:1]
            state_scr[h] = state * dec_col + kv
            on = _rms(o, gon_ref[...])
            rr = r_ref[rows, vsl]
            o_ref[rows, vsl] = (on * (rr * jax.nn.sigmoid(rr))).astype(o_ref.dtype)
        return carry

    lax.fori_loop(0, n_chunks, chunk, 0)


def _gla_chunk(q, k, v, r, la, g_on, batch, seq, tc):
    hk = GLA_HEADS * GLA_DK
    hv = GLA_HEADS * GLA_DV
    ns = seq // tc
    row = lambda c: pl.BlockSpec((tc, c), lambda b, s: (b * ns + s, 0))
    return pl.pallas_call(
        _gla_chunk_kernel,
        grid=(batch, ns),
        in_specs=[row(hk), row(hk), row(hv), row(hv), row(hk), _full((1, GLA_DV))],
        out_specs=row(hv),
        out_shape=jax.ShapeDtypeStruct((batch * seq, hv), BF16),
        scratch_shapes=[pltpu.VMEM((GLA_HEADS, GLA_DK, GLA_DV), F32)],
        compiler_params=_params(("parallel", "arbitrary")),
        name="gla_chunk",
    )(q, k, v, r, la, g_on)


def _prep_mla(w_down, g_q_lat, w_uq, g_kv_lat, w_ukv, g_qn, g_kn):
    d = w_down.shape[0]
    lat = MLA_Q_RANK + MLA_KV_RANK
    z = lambda c: jnp.zeros((d, c), F32)
    wd = jnp.concatenate([w_down[:, :lat], z(MLA_NOPE), w_down[:, lat:], z(LANES - MLA_QK)], axis=1)
    pad_head = LANES - MLA_QK
    wuq = jnp.pad(w_uq.reshape(MLA_Q_RANK, MLA_HEADS, MLA_QK), ((0, 0), (0, 0), (0, pad_head)))
    wukv = w_ukv.reshape(MLA_KV_RANK, MLA_HEADS, MLA_NOPE + MLA_V)
    wuk = jnp.pad(wukv[:, :, :MLA_NOPE], ((0, 0), (0, 0), (0, LANES - MLA_NOPE)))
    wuv = jnp.pad(wukv[:, :, MLA_NOPE:], ((0, 0), (0, 0), (0, LANES - MLA_V)))
    half = MLA_ROPE // 2
    inv_freq = ROPE_THETA ** (-jnp.arange(half, dtype=F32) / half)
    zero = lambda c: jnp.zeros((c,), F32)
    vone = jnp.tile(jnp.concatenate([zero(MLA_V), jnp.ones((LANES - MLA_V,), F32)]), MLA_HEADS)
    freq = jnp.concatenate([zero(MLA_NOPE), inv_freq, inv_freq, zero(pad_head)])
    ma = jnp.concatenate([zero(MLA_NOPE), -jnp.ones((half,), F32), zero(half + pad_head)])
    mb = jnp.concatenate([zero(MLA_NOPE + half), jnp.ones((half,), F32), zero(pad_head)])
    scale = MLA_QK ** -0.5 * math.log2(math.e)
    return {
        "wd": wd.astype(BF16),
        "gql": g_q_lat.reshape(1, -1), "gkl": g_kv_lat.reshape(1, -1),
        "wuq": wuq.reshape(MLA_Q_RANK, MLA_HEADS * LANES).astype(BF16),
        "wuk": wuk.reshape(MLA_KV_RANK, MLA_HEADS * LANES).astype(BF16),
        "wuv": wuv.reshape(MLA_KV_RANK, MLA_HEADS * LANES).astype(BF16),
        "vone": vone.reshape(1, MLA_HEADS * LANES),
        "gqn": (jnp.pad(g_qn, (0, pad_head)) * scale).reshape(1, LANES),
        "gkn": jnp.pad(g_kn, (0, pad_head)).reshape(1, LANES),
        "freq": freq.reshape(1, LANES), "ma": ma.reshape(1, LANES), "mb": mb.reshape(1, LANES),
    }


def _prep_gla(w_in, w_g2, b_g):
    pad = LANES - GLA_GATE_RANK
    return {
        "win": jnp.pad(w_in, ((0, 0), (0, pad))).astype(BF16),
        "wg2": jnp.pad(w_g2, ((0, pad), (0, 0))),
        "bg": b_g.reshape(1, -1),
    }


def _prep_peer(w_query, sub_keys, u_tab, v_tab):
    zk = jnp.zeros((PEER_NKEYS, PEER_HALF), F32)
    return {
        "wq": w_query,
        "k0": jnp.concatenate([sub_keys[0], zk], axis=1),
        "k1": jnp.concatenate([zk, sub_keys[1]], axis=1),
        "u": u_tab.astype(BF16),
        "vt": v_tab.T.astype(BF16),
    }


def _peer_layer(res, a, w_o, g, pw, tiles):
    h, xt, s0, s1, thr = _peer_route(res, a, w_o, g.reshape(1, -1), pw["wq"], pw["k0"], pw["k1"],
                                     tiles["route"])
    return _peer_dense(xt, s0, s1, thr, pw["u"], pw["vt"], h, tiles["dense_t"], tiles["dense_e"])


def _tiles(seq):
    pick = lambda want: math.gcd(want, seq)
    return {"proj": pick(512), "attn": pick(2048), "route": pick(256), "dense_t": pick(1024),
            "dense_e": 8 * PEER_NKEYS, "gla": pick(512)}


def kernel(x, positions, attn_norm_g, ffn_norm_g, mla_w_down, mla_g_q_lat, mla_w_uq, mla_g_kv_lat,
           mla_w_ukv, mla_g_qn, mla_g_kn, mla_w_o, gla_w_in, gla_w_g2, gla_b_g, gla_g_on, gla_w_o,
           peer_w_query, peer_sub_keys, peer_u, peer_v):
    batch, seq, d = x.shape
    n = batch * seq
    depth = attn_norm_g.shape[0]
    tiles = _tiles(seq)
    h = x.reshape(n, d)
    pos = positions.reshape(n, 1).astype(F32)
    for i in range(depth):
        j = i // 2
        g_in = attn_norm_g[i].reshape(1, d)
        if i % 2 == 0:
            w = _prep_mla(mla_w_down[j], mla_g_q_lat[j], mla_w_uq[j], mla_g_kv_lat[j], mla_w_ukv[j],
                          mla_g_qn[j], mla_g_kn[j])
            q, k, v = _mla_proj(h, pos, g_in, w, tiles["proj"])
            a = _attention(q, k, v, batch, seq, tiles["attn"])
            w_o = mla_w_o[j].astype(BF16)
        else:
            w = _prep_gla(gla_w_in[j], gla_w_g2[j], gla_b_g[j])
            q, k, v, r, la = _gla_proj(h, g_in, w, tiles["proj"])
            a = _gla_chunk(q, k, v, r, la, gla_g_on[j].reshape(1, -1), batch, seq, tiles["gla"])
            w_o = gla_w_o[j].astype(BF16)
        pw = _prep_peer(peer_w_query[i], peer_sub_keys[i], peer_u[i], peer_v[i])
        h = _peer_layer(h, a, w_o, ffn_norm_g[i], pw, tiles)
    return h.reshape(batch, seq, d)
```

```python
import math

import jax
import jax.numpy as jnp
import numpy as np
from jax import lax
from jax.experimental import pallas as pl
from jax.experimental.pallas import tpu as pltpu

F32 = jnp.float32
BF16 = jnp.bfloat16
HIGHEST = lax.Precision.HIGHEST

LANES = 128
NORM_EPS = 1e-6
VMEM_LIMIT = 56 * 1024 * 1024

MLA_HEADS = 16
MLA_Q_RANK = 384
MLA_KV_RANK = 256
MLA_NOPE = 64
MLA_ROPE = 32
MLA_QK = MLA_NOPE + MLA_ROPE
MLA_V = 64
ROPE_THETA = 10000.0
GLA_HEADS = 4
GLA_DK = 128
GLA_DV = 256
GLA_GATE_RANK = 16
GLA_TAU = 16.0
GLA_CHUNK = 64
PEER_HEADS = 8
PEER_NKEYS = 128
PEER_HALF = 64
PEER_TOPK = 16
NEG_INF = float("-inf")
PEER_STEP_KEYS = 8
ATTN_ROW_CHUNK = 1024


def _params(sem):
    return pltpu.CompilerParams(dimension_semantics=sem, vmem_limit_bytes=VMEM_LIMIT)


def _rms(x, g, n=None):
    n = x.shape[-1] if n is None else n
    ss = jnp.sum(x * x, axis=-1, keepdims=True)
    return x * lax.rsqrt(ss * (1.0 / n) + NORM_EPS) * g


def _full(shape):
    nd = len(shape)
    return pl.BlockSpec(shape, lambda *_: (0,) * nd)


def _mla_proj_kernel(x_ref, pos_ref, gin_ref, wd_ref, gql_ref, wuq_ref, gkl_ref, wuk_ref, wuv_ref,
                     gqn_ref, gkn_ref, freq_ref, ma_ref, mb_ref, vone_ref, q_ref, k_ref, v_ref):
    x = x_ref[...]
    hn = _rms(x, gin_ref[...])
    down = jnp.dot(hn.astype(BF16), wd_ref[...], preferred_element_type=F32)
    c_q = _rms(down[:, :MLA_Q_RANK], gql_ref[...]).astype(BF16)
    c_kv = _rms(down[:, MLA_Q_RANK:MLA_Q_RANK + MLA_KV_RANK], gkl_ref[...]).astype(BF16)
    kr = down[:, MLA_Q_RANK + MLA_KV_RANK:]
    qf = jnp.dot(c_q, wuq_ref[...], preferred_element_type=F32)
    kf = jnp.dot(c_kv, wuk_ref[...], preferred_element_type=F32)
    v_ref[...] = (jnp.dot(c_kv, wuv_ref[...], preferred_element_type=F32) + vone_ref[...]).astype(v_ref.dtype)

    ang = pos_ref[...] * freq_ref[...]
    cos = jnp.cos(ang)
    sin = jnp.sin(ang)
    sa = sin * ma_ref[...]
    sb = sin * mb_ref[...]
    half = MLA_ROPE // 2

    def head(t, g):
        t = _rms(t, g, MLA_QK)
        return t * cos + pltpu.roll(t, LANES - half, 1) * sa + pltpu.roll(t, half, 1) * sb

    for h in range(MLA_HEADS):
        sl = slice(h * LANES, (h + 1) * LANES)
        q_ref[:, sl] = head(qf[:, sl], gqn_ref[...]).astype(q_ref.dtype)
        k_ref[:, sl] = head(kf[:, sl] + kr, gkn_ref[...]).astype(k_ref.dtype)


def _mla_proj(x2, pos2, g_in, w, tm):
    n, d = x2.shape
    hq = MLA_HEADS * LANES
    row = lambda c: pl.BlockSpec((tm, c), lambda i: (i, 0))
    return pl.pallas_call(
        _mla_proj_kernel,
        grid=(n // tm,),
        in_specs=[row(d), row(1), _full((1, d)), _full(w["wd"].shape), _full((1, MLA_Q_RANK)),
                  _full(w["wuq"].shape), _full((1, MLA_KV_RANK)), _full(w["wuk"].shape),
                  _full(w["wuv"].shape), _full((1, LANES)), _full((1, LANES)), _full((1, LANES)),
                  _full((1, LANES)), _full((1, LANES)), _full((1, hq))],
        out_specs=[row(hq), row(hq), row(hq)],
        out_shape=[jax.ShapeDtypeStruct((n, hq), BF16)] * 3,
        compiler_params=_params(("parallel",)),
        name="mla_proj",
    )(x2, pos2, g_in, w["wd"], w["gql"], w["wuq"], w["gkl"], w["wuk"], w["wuv"], w["gqn"], w["gkn"],
      w["freq"], w["ma"], w["mb"], w["vone"])


def _attn_kernel(qi_ref, ki_ref, q_ref, k_ref, v_ref, o_ref, m_scr, acc_scr):
    p_id = pl.program_id(2)
    qi = qi_ref[p_id]
    ki = ki_ref[p_id]
    tq = q_ref.shape[0]
    tk = k_ref.shape[0]
    rq = min(ATTN_ROW_CHUNK, tq)

    @pl.when(ki == 0)
    def _():
        m_scr[...] = jnp.full(m_scr.shape, NEG_INF, F32)
        acc_scr[...] = jnp.zeros(acc_scr.shape, F32)

    def step(diagonal):
        for hh in range(2):
            sl = slice(hh * LANES, (hh + 1) * LANES)
            for r in range(tq // rq):
                rows = slice(r * rq, (r + 1) * rq)
                nk = (r + 1) * rq if diagonal else tk
                s = lax.dot_general(q_ref[rows, sl], k_ref[:nk, sl], (((1,), (1,)), ((), ())),
                                    preferred_element_type=F32)
                if diagonal:
                    row = lax.broadcasted_iota(jnp.int32, (rq, rq), 0)
                    col = lax.broadcasted_iota(jnp.int32, (rq, rq), 1)
                    tail = jnp.where(col <= row, s[:, r * rq:], NEG_INF)
                    s = tail if r == 0 else jnp.concatenate([s[:, :r * rq], tail], axis=1)
                m_prev = m_scr[hh, rows]
                m_new = jnp.maximum(m_prev, jnp.max(s, axis=-1, keepdims=True))
                alpha = jnp.exp2(m_prev - m_new)
                p = jnp.concatenate(
                    [jnp.exp2(s[:, c * LANES:(c + 1) * LANES] - m_new).astype(BF16)
                     for c in range(nk // LANES)], axis=1)
                pv = jnp.dot(p, v_ref[:nk, sl], preferred_element_type=F32)
                acc_scr[hh, rows] = alpha * acc_scr[hh, rows] + pv
                m_scr[hh, rows] = m_new

    @pl.when(ki < qi)
    def _():
        step(False)

    @pl.when(ki == qi)
    def _():
        step(True)
        lane = lax.broadcasted_iota(jnp.int32, (tq, LANES), 1)
        a0 = acc_scr[0]
        a1 = acc_scr[1]
        o0 = a0 / pltpu.roll(a0, MLA_V, 1)
        o1 = pltpu.roll(a1, MLA_V, 1) / a1
        o_ref[...] = jnp.where(lane < MLA_V, o0, o1).astype(o_ref.dtype)


def _attention(q, k, v, batch, seq, tq):
    nq = seq // tq
    qi_tab = np.array([qi for qi in range(nq) for _ in range(qi + 1)], np.int32)
    ki_tab = np.array([ki for qi in range(nq) for ki in range(qi + 1)], np.int32)
    npairs = len(qi_tab)
    hp = MLA_HEADS // 2
    blk = lambda tab: pl.BlockSpec((tq, 2 * LANES), lambda b, h, p, qt, kt: (b * nq + (qt, kt)[tab][p], h))
    grid_spec = pltpu.PrefetchScalarGridSpec(
        num_scalar_prefetch=2,
        grid=(batch, hp, npairs),
        in_specs=[blk(0), blk(1), blk(1)],
        out_specs=pl.BlockSpec((tq, 2 * MLA_V), lambda b, h, p, qt, kt: (b * nq + qt[p], h)),
        scratch_shapes=[pltpu.VMEM((2, tq, LANES), F32), pltpu.VMEM((2, tq, LANES), F32)],
    )
    return pl.pallas_call(
        _attn_kernel,
        grid_spec=grid_spec,
        out_shape=jax.ShapeDtypeStruct((batch * seq, MLA_HEADS * MLA_V), BF16),
        compiler_params=_params(("parallel", "parallel", "arbitrary")),
        name="mla_attention",
    )(jnp.asarray(qi_tab), jnp.asarray(ki_tab), q, k, v)


def _sorted_top(cur, n, val_scr, idx_scr=None):
    rows = lax.broadcasted_iota(jnp.int32, cur.shape, 0).astype(F32)
    for r in range(n):
        m = jnp.max(cur, axis=0, keepdims=True)
        val_scr[r:r + 1, :] = m
        if idx_scr is None:
            cur = jnp.where(cur >= m, NEG_INF, cur)
        else:
            first = jnp.min(jnp.where(cur == m, rows, float(cur.shape[0])), axis=0, keepdims=True)
            idx_scr[r:r + 1, :] = first
            cur = jnp.where(rows == first, NEG_INF, cur)


def _peer_route_kernel(res_ref, a_ref, wo_ref, g_ref, wq_ref, k0_ref, k1_ref,
                       h_ref, x_ref, idx_ref, a2_ref, th_ref, s1_ref, b2_ref,
                       a_scr, b_scr, i_scr, idx_all, a2_all):
    tr = res_ref.shape[0]
    h = res_ref[...] + jnp.dot(a_ref[...], wo_ref[...], preferred_element_type=F32)
    h_ref[...] = h
    hn = _rms(h, g_ref[...])
    x_ref[...] = (hn * (1.0 / math.sqrt(2.0))).astype(x_ref.dtype)
    q = jnp.dot(hn, wq_ref[...], precision=HIGHEST, preferred_element_type=F32)
    nk = PEER_TOPK + 1
    rows = lax.broadcasted_iota(jnp.int32, (8, tr), 0)
    a_scr[...] = jnp.full(a_scr.shape, NEG_INF, F32)
    b_scr[...] = jnp.full(b_scr.shape, NEG_INF, F32)
    contract_last = (((1,), (1,)), ((), ()))
    log2e = math.log2(math.e)
    for hd in range(PEER_HEADS):
        qh = q[:, hd * LANES:(hd + 1) * LANES]
        s0 = lax.dot_general(k0_ref[...], qh, contract_last, precision=HIGHEST,
                             preferred_element_type=F32)
        s1 = lax.dot_general(k1_ref[...], qh, contract_last, precision=HIGHEST,
                             preferred_element_type=F32)
        _sorted_top(s0, nk, a_scr, i_scr)
        _sorted_top(s1, nk, b_scr)
        pieces = [b_scr[...] + a_scr[0:1, :]]
        r = 1
        while nk // (r + 1) >= 2:
            pieces.append(jnp.where(rows < nk // (r + 1), b_scr[0:8, :] + a_scr[r:r + 1, :], NEG_INF))
            r += 1
        assert r == 8
        pieces.append(a_scr[8:24, :] + b_scr[0:1, :])
        cand = jnp.concatenate(pieces, axis=0)
        cur = cand
        best = jnp.max(cur, axis=0, keepdims=True)
        for _ in range(PEER_TOPK - 1):
            m = jnp.max(cur, axis=0, keepdims=True)
            cur = jnp.where(cur >= m, NEG_INF, cur)
        v16 = jnp.max(cur, axis=0, keepdims=True)
        cur = jnp.where(cur >= v16, NEG_INF, cur)
        v17 = jnp.max(cur, axis=0, keepdims=True)
        tau = 0.5 * (v16 + v17)
        z = jnp.sum(jnp.where(cand >= tau, jnp.exp(cand - best), 0.0), axis=0, keepdims=True)
        a_max = a_scr[0:1, :]
        a_rel = (a_scr[0:PEER_TOPK, :] - a_max) * log2e
        s1_rel = (s1 - (best - a_max + jnp.log(z) + 0.5 * math.log(2.0))) * log2e
        thr = (tau - (best + jnp.log(z) + 0.5 * math.log(2.0))) * log2e
        slots = slice(hd * PEER_TOPK, (hd + 1) * PEER_TOPK)
        idx_all[slots, :] = i_scr[0:PEER_TOPK, :]
        a2_all[slots, :] = jnp.exp2(a_rel)
        th_ref[slots, :] = thr - a_rel
        s1t = s1_rel.T
        s1_ref[hd] = s1t
        b2_ref[hd] = jnp.exp2(s1t)
    idx_ref[...] = idx_all[...].T
    a2_ref[...] = a2_all[...].T


def _peer_route(res, a, w_o, g, wq, k0, k1, tr):
    n, d = res.shape
    da = a.shape[1]
    nh = PEER_HEADS
    ns = nh * PEER_TOPK
    row = lambda c: pl.BlockSpec((tr, c), lambda i: (i, 0))
    per_head = pl.BlockSpec((nh, tr, PEER_NKEYS), lambda i: (0, i, 0))
    return pl.pallas_call(
        _peer_route_kernel,
        grid=(n // tr,),
        in_specs=[row(d), row(da), _full(w_o.shape), _full((1, d)), _full(wq.shape), _full(k0.shape),
                  _full(k1.shape)],
        out_specs=[row(d), row(d), row(ns), row(ns), pl.BlockSpec((ns, tr), lambda i: (0, i)),
                   per_head, per_head],
        out_shape=[jax.ShapeDtypeStruct((n, d), F32), jax.ShapeDtypeStruct((n, d), BF16),
                   jax.ShapeDtypeStruct((n, ns), F32), jax.ShapeDtypeStruct((n, ns), F32),
                   jax.ShapeDtypeStruct((ns, n), F32),
                   jax.ShapeDtypeStruct((nh, n, PEER_NKEYS), F32),
                   jax.ShapeDtypeStruct((nh, n, PEER_NKEYS), F32)],
        scratch_shapes=[pltpu.VMEM((24, tr), F32), pltpu.VMEM((24, tr), F32), pltpu.VMEM((24, tr), F32),
                        pltpu.VMEM((ns, tr), F32), pltpu.VMEM((ns, tr), F32)],
        compiler_params=_params(("parallel",)),
        name="peer_route",
    )(res, a, w_o, g, wq, k0, k1)


def _peer_weights_kernel(idx_ref, a2_ref, th_ref, s1_ref, b2_ref, g_ref):
    tg = idx_ref.shape[0]
    ns = th_ref.shape[0]
    sub = lax.broadcasted_iota(jnp.int32, (PEER_NKEYS, ns), 0).astype(F32)

    def head_rows(ref, t):
        return jnp.concatenate([jnp.broadcast_to(ref[hd, t:t + 1, :], (PEER_TOPK, PEER_NKEYS))
                                for hd in range(PEER_HEADS)], axis=0)

    for t in range(tg):
        first = jnp.where(sub == idx_ref[t:t + 1, :], a2_ref[t:t + 1, :], 0.0).astype(BF16)
        thr = jnp.broadcast_to(th_ref[:, t:t + 1], (ns, PEER_NKEYS))
        second = jnp.where(head_rows(s1_ref, t) >= thr, head_rows(b2_ref, t), 0.0).astype(BF16)
        gt = jnp.dot(first, second, preferred_element_type=F32)
        for blk in range(PEER_NKEYS // PEER_STEP_KEYS):
            g_ref[blk, t] = gt[blk * PEER_STEP_KEYS:(blk + 1) * PEER_STEP_KEYS, :]


def _peer_weights(idx, a2, th, s1, b2, tg):
    n, ns = idx.shape
    nh = PEER_HEADS
    nb = PEER_NKEYS // PEER_STEP_KEYS
    row = pl.BlockSpec((tg, ns), lambda i: (i, 0))
    per_head = pl.BlockSpec((nh, tg, PEER_NKEYS), lambda i: (0, i, 0))
    return pl.pallas_call(
        _peer_weights_kernel,
        grid=(n // tg,),
        in_specs=[row, row, pl.BlockSpec((ns, tg), lambda i: (0, i)), per_head, per_head],
        out_specs=pl.BlockSpec((nb, tg, PEER_STEP_KEYS, PEER_NKEYS), lambda i: (0, i, 0, 0)),
        out_shape=jax.ShapeDtypeStruct((nb, n, PEER_STEP_KEYS, PEER_NKEYS), F32),
        compiler_params=_params(("parallel",)),
        name="peer_weights",
    )(idx, a2, th, s1, b2)


def _peer_dense_kernel(x_ref, g_ref, u_ref, v_ref, h_ref, o_ref, acc_scr):
    e = pl.program_id(1)
    ne = pl.num_programs(1)
    t = x_ref.shape[0]
    rows_per_step = PEER_STEP_KEYS

    @pl.when(e == 0)
    def _():
        acc_scr[...] = jnp.zeros(acc_scr.shape, F32)

    y = lax.dot_general(x_ref[...], u_ref[...], (((1,), (1,)), ((), ())),
                        preferred_element_type=F32)
    ps = []
    for ii in range(rows_per_step):
        yc = y[:, ii * PEER_NKEYS:(ii + 1) * PEER_NKEYS]
        gi = g_ref[pl.ds(ii, t, stride=rows_per_step), :]
        ps.append((gi * (yc * (1.0 + lax.erf(yc)))).astype(BF16))
    acc_scr[...] += jnp.dot(jnp.concatenate(ps, axis=1), v_ref[...], preferred_element_type=F32)

    @pl.when(e == ne - 1)
    def _():
        o_ref[...] = h_ref[...] + acc_scr[...]


def _peer_dense(x, g, u, v, h, t, eb):
    n, d = x.shape
    ne = u.shape[0] // eb
    tok = lambda c: pl.BlockSpec((t, c), lambda i, e: (i, 0))
    return pl.pallas_call(
        _peer_dense_kernel,
        grid=(n // t, ne),
        in_specs=[tok(d),
                  pl.BlockSpec((None, t * PEER_STEP_KEYS, PEER_NKEYS), lambda i, e: (e, i, 0)),
                  pl.BlockSpec((eb, d), lambda i, e: (e, 0)),
                  pl.BlockSpec((eb, d), lambda i, e: (e, 0)),
                  pl.BlockSpec((t, d), lambda i, e: (i, 0), pipeline_mode=pl.Buffered(1))],
        out_specs=tok(d),
        out_shape=jax.ShapeDtypeStruct((n, d), F32),
        scratch_shapes=[pltpu.VMEM((t, d), F32)],
        compiler_params=_params(("parallel", "arbitrary")),
        name="peer_dense",
    )(x, g, u, v, h)


def _gla_proj_kernel(x_ref, gin_ref, win_ref, wg2_ref, bg_ref, q_ref, k_ref, v_ref, r_ref, la_ref):
    hk = GLA_HEADS * GLA_DK
    hv = GLA_HEADS * GLA_DV
    hn = _rms(x_ref[...], gin_ref[...])
    proj = jnp.dot(hn.astype(BF16), win_ref[...], preferred_element_type=F32)
    q_ref[...] = proj[:, :hk] * (GLA_DK ** -0.5)
    k_ref[...] = proj[:, hk:2 * hk]
    v_ref[...] = proj[:, 2 * hk:2 * hk + hv]
    r_ref[...] = proj[:, 2 * hk + hv:2 * hk + 2 * hv]
    g_low = proj[:, 2 * hk + 2 * hv:]
    z = jnp.dot(g_low, wg2_ref[...], precision=HIGHEST, preferred_element_type=F32) + bg_ref[...]
    la_ref[...] = (jnp.minimum(z, 0.0) - jnp.log1p(jnp.exp(-jnp.abs(z)))) * (1.0 / GLA_TAU)


def _gla_proj(x2, g_in, w, tm):
    n, d = x2.shape
    hk = GLA_HEADS * GLA_DK
    hv = GLA_HEADS * GLA_DV
    row = lambda c: pl.BlockSpec((tm, c), lambda i: (i, 0))
    return pl.pallas_call(
        _gla_proj_kernel,
        grid=(n // tm,),
        in_specs=[row(d), _full((1, d)), _full(w["win"].shape), _full(w["wg2"].shape), _full((1, hk))],
        out_specs=[row(hk), row(hk), row(hv), row(hv), row(hk)],
        out_shape=[jax.ShapeDtypeStruct((n, hk), F32), jax.ShapeDtypeStruct((n, hk), F32),
                   jax.ShapeDtypeStruct((n, hv), F32), jax.ShapeDtypeStruct((n, hv), F32),
                   jax.ShapeDtypeStruct((n, hk), F32)],
        compiler_params=_params(("parallel",)),
        name="gla_proj",
    )(x2, g_in, w["win"], w["wg2"], w["bg"])


def _gla_chunk_kernel(q_ref, k_ref, v_ref, r_ref, la_ref, gon_ref, o_ref, state_scr):
    c_len = GLA_CHUNK
    n_chunks = q_ref.shape[0] // c_len

    @pl.when(pl.program_id(1) == 0)
    def _():
        state_scr[...] = jnp.zeros(state_scr.shape, F32)

    ri = lax.broadcasted_iota(jnp.int32, (c_len, c_len), 0)
    ci = lax.broadcasted_iota(jnp.int32, (c_len, c_len), 1)
    causal = ci <= ri
    tril = causal.astype(F32)
    contract_last = (((1,), (1,)), ((), ()))

    def chunk(c, carry):
        r0 = pl.multiple_of(c * c_len, c_len)
        rows = pl.ds(r0, c_len)
        for h in range(GLA_HEADS):
            ksl = slice(h * GLA_DK, (h + 1) * GLA_DK)
            vsl = slice(h * GLA_DV, (h + 1) * GLA_DV)
            qc = q_ref[rows, ksl]
            kc = k_ref[rows, ksl]
            vc = v_ref[rows, vsl].astype(BF16)
            b = jnp.dot(tril, la_ref[rows, ksl], precision=HIGHEST, preferred_element_type=F32)
            b_last = b[c_len - 1:c_len, :]
            q_dec = (qc * jnp.exp(b)).astype(BF16)
            k_inv = (kc * jnp.exp(-b)).astype(BF16)
            k_rem = kc * jnp.exp(b_last - b)
            attn = lax.dot_general(q_dec, k_inv, contract_last, preferred_element_type=F32)
            attn = jnp.where(causal, attn, 0.0).astype(BF16)
            state = state_scr[h]
            o = (jnp.dot(attn, vc, preferred_element_type=F32)
                 + jnp.dot(q_dec, state.astype(BF16), preferred_element_type=F32))
            kv = jnp.dot(k_rem.T.astype(BF16), vc, preferred_element_type=F32)
            dec_col = jnp.broadcast_to(jnp.exp(b_last), (8, GLA_DK)).T[:, 0:1]
            state_scr[h] = state * dec_col + kv
            on = _rms(o, gon_ref[...])
            rr = r_ref[rows, vsl]
            o_ref[rows, vsl] = (on * (rr * jax.nn.sigmoid(rr))).astype(o_ref.dtype)
        return carry

    lax.fori_loop(0, n_chunks, chunk, 0)


def _gla_chunk(q, k, v, r, la, g_on, batch, seq, tc):
    hk = GLA_HEADS * GLA_DK
    hv = GLA_HEADS * GLA_DV
    ns = seq // tc
    row = lambda c: pl.BlockSpec((tc, c), lambda b, s: (b * ns + s, 0))
    return pl.pallas_call(
        _gla_chunk_kernel,
        grid=(batch, ns),
        in_specs=[row(hk), row(hk), row(hv), row(hv), row(hk), _full((1, GLA_DV))],
        out_specs=row(hv),
        out_shape=jax.ShapeDtypeStruct((batch * seq, hv), BF16),
        scratch_shapes=[pltpu.VMEM((GLA_HEADS, GLA_DK, GLA_DV), F32)],
        compiler_params=_params(("parallel", "arbitrary")),
        name="gla_chunk",
    )(q, k, v, r, la, g_on)


def _prep_mla(w_down, g_q_lat, w_uq, g_kv_lat, w_ukv, g_qn, g_kn):
    d = w_down.shape[0]
    lat = MLA_Q_RANK + MLA_KV_RANK
    z = lambda c: jnp.zeros((d, c), F32)
    wd = jnp.concatenate([w_down[:, :lat], z(MLA_NOPE), w_down[:, lat:], z(LANES - MLA_QK)], axis=1)
    pad_head = LANES - MLA_QK
    wuq = jnp.pad(w_uq.reshape(MLA_Q_RANK, MLA_HEADS, MLA_QK), ((0, 0), (0, 0), (0, pad_head)))
    wukv = w_ukv.reshape(MLA_KV_RANK, MLA_HEADS, MLA_NOPE + MLA_V)
    wuk = jnp.pad(wukv[:, :, :MLA_NOPE], ((0, 0), (0, 0), (0, LANES - MLA_NOPE)))
    wuv = jnp.pad(wukv[:, :, MLA_NOPE:], ((0, 0), (0, 0), (0, LANES - MLA_V)))
    half = MLA_ROPE // 2
    inv_freq = ROPE_THETA ** (-jnp.arange(half, dtype=F32) / half)
    zero = lambda c: jnp.zeros((c,), F32)
    vone = jnp.tile(jnp.concatenate([zero(MLA_V), jnp.ones((LANES - MLA_V,), F32)]), MLA_HEADS)
    freq = jnp.concatenate([zero(MLA_NOPE), inv_freq, inv_freq, zero(pad_head)])
    ma = jnp.concatenate([zero(MLA_NOPE), -jnp.ones((half,), F32), zero(half + pad_head)])
    mb = jnp.concatenate([zero(MLA_NOPE + half), jnp.ones((half,), F32), zero(pad_head)])
    scale = MLA_QK ** -0.5 * math.log2(math.e)
    return {
        "wd": wd.astype(BF16),
        "gql": g_q_lat.reshape(1, -1), "gkl": g_kv_lat.reshape(1, -1),
        "wuq": wuq.reshape(MLA_Q_RANK, MLA_HEADS * LANES).astype(BF16),
        "wuk": wuk.reshape(MLA_KV_RANK, MLA_HEADS * LANES).astype(BF16),
        "wuv": wuv.reshape(MLA_KV_RANK, MLA_HEADS * LANES).astype(BF16),
        "vone": vone.reshape(1, MLA_HEADS * LANES),
        "gqn": (jnp.pad(g_qn, (0, pad_head)) * scale).reshape(1, LANES),
        "gkn": jnp.pad(g_kn, (0, pad_head)).reshape(1, LANES),
        "freq": freq.reshape(1, LANES), "ma": ma.reshape(1, LANES), "mb": mb.reshape(1, LANES),
    }


def _prep_gla(w_in, w_g2, b_g):
    pad = LANES - GLA_GATE_RANK
    return {
        "win": jnp.pad(w_in, ((0, 0), (0, pad))).astype(BF16),
        "wg2": jnp.pad(w_g2, ((0, pad), (0, 0))),
        "bg": b_g.reshape(1, -1),
    }


def _prep_peer(w_query, sub_keys, u_tab, v_tab):
    zk = jnp.zeros((PEER_NKEYS, PEER_HALF), F32)
    return {
        "wq": w_query,
        "k0": jnp.concatenate([sub_keys[0], zk], axis=1),
        "k1": jnp.concatenate([zk, sub_keys[1]], axis=1),
        "u": u_tab.astype(BF16),
        "v": v_tab.astype(BF16),
    }


def _peer_layer(res, a, w_o, g, pw, tiles):
    h, xn, idx, a2, th, s1, b2 = _peer_route(res, a, w_o, g.reshape(1, -1), pw["wq"], pw["k0"], pw["k1"],
                                             tiles["route"])
    gate = _peer_weights(idx, a2, th, s1, b2, tiles["weights"])
    gate = gate.reshape(gate.shape[0], -1, PEER_NKEYS)
    return _peer_dense(xn, gate, pw["u"], pw["v"], h, tiles["dense_t"], PEER_STEP_KEYS * PEER_NKEYS)


def _tiles(seq):
    pick = lambda want: math.gcd(want, seq)
    return {"proj": pick(512), "attn": pick(2048), "route": pick(256), "weights": pick(128),
            "dense_t": pick(1024), "gla": pick(512)}


def kernel(x, positions, attn_norm_g, ffn_norm_g, mla_w_down, mla_g_q_lat, mla_w_uq, mla_g_kv_lat,
           mla_w_ukv, mla_g_qn, mla_g_kn, mla_w_o, gla_w_in, gla_w_g2, gla_b_g, gla_g_on, gla_w_o,
           peer_w_query, peer_sub_keys, peer_u, peer_v):
    batch, seq, d = x.shape
    n = batch * seq
    depth = attn_norm_g.shape[0]
    tiles = _tiles(seq)
    h = x.reshape(n, d)
    pos = positions.reshape(n, 1).astype(F32)
    for i in range(depth):
        j = i // 2
        g_in = attn_norm_g[i].reshape(1, d)
        if i % 2 == 0:
            w = _prep_mla(mla_w_down[j], mla_g_q_lat[j], mla_w_uq[j], mla_g_kv_lat[j], mla_w_ukv[j],
                          mla_g_qn[j], mla_g_kn[j])
            q, k, v = _mla_proj(h, pos, g_in, w, tiles["proj"])
            a = _attention(q, k, v, batch, seq, tiles["attn"])
            w_o = mla_w_o[j].astype(BF16)
        else:
            w = _prep_gla(gla_w_in[j], gla_w_g2[j], gla_b_g[j])
            q, k, v, r, la = _gla_proj(h, g_in, w, tiles["proj"])
            a = _gla_chunk(q, k, v, r, la, gla_g_on[j].reshape(1, -1), batch, seq, tiles["gla"])
            w_o = gla_w_o[j].astype(BF16)
        pw = _prep_peer(peer_w_query[i], peer_sub_keys[i], peer_u[i], peer_v[i])
        h = _peer_layer(h, a, w_o, ffn_norm_g[i], pw, tiles)
    return h.reshape(batch, seq, d)
```

```python
import math

import jax
import jax.numpy as jnp
import numpy as np
from jax import lax
from jax.experimental import pallas as pl
from jax.experimental.pallas import tpu as pltpu

F32 = jnp.float32
BF16 = jnp.bfloat16
HIGHEST = lax.Precision.HIGHEST

LANES = 128
NORM_EPS = 1e-6
VMEM_LIMIT = 56 * 1024 * 1024

MLA_HEADS = 16
MLA_Q_RANK = 384
MLA_KV_RANK = 256
MLA_NOPE = 64
MLA_ROPE = 32
MLA_QK = MLA_NOPE + MLA_ROPE
MLA_V = 64
ROPE_THETA = 10000.0
GLA_HEADS = 4
GLA_DK = 128
GLA_DV = 256
GLA_GATE_RANK = 16
GLA_TAU = 16.0
GLA_CHUNK = 64
PEER_HEADS = 8
PEER_NKEYS = 128
PEER_HALF = 64
PEER_TOPK = 16
NEG_INF = float("-inf")
PEER_STEP_KEYS = 8
GLA_UNROLL = 2
ATTN_ROW_CHUNK = 1024


def _params(sem):
    return pltpu.CompilerParams(dimension_semantics=sem, vmem_limit_bytes=VMEM_LIMIT)


def _rms(x, g, n=None):
    n = x.shape[-1] if n is None else n
    ss = jnp.sum(x * x, axis=-1, keepdims=True)
    return x * lax.rsqrt(ss * (1.0 / n) + NORM_EPS) * g


def _split3(x):
    hi = x.astype(BF16)
    lo = (x - hi.astype(F32)).astype(BF16)
    return jnp.concatenate([hi, lo, hi], axis=-1)


def _split3_const(w, axis):
    bits = lax.bitcast_convert_type(w, jnp.uint32) & jnp.uint32(0xFFFF0000)
    hi = lax.bitcast_convert_type(bits, F32)
    lo = (w - hi).astype(BF16)
    hi = hi.astype(BF16)
    return jnp.concatenate([hi, hi, lo], axis=axis)


def _full(shape):
    nd = len(shape)
    return pl.BlockSpec(shape, lambda *_: (0,) * nd)


def _mla_proj_kernel(x_ref, pos_ref, gin_ref, wd_ref, gql_ref, wuq_ref, gkl_ref, wuk_ref, wuv_ref,
                     gqn_ref, gkn_ref, freq_ref, rotw_ref, vone_ref, q_ref, k_ref, v_ref):
    x = x_ref[...]
    hn = _rms(x, gin_ref[...])
    down = jnp.dot(hn.astype(BF16), wd_ref[...], preferred_element_type=F32)
    c_q = _rms(down[:, :MLA_Q_RANK], gql_ref[...]).astype(BF16)
    c_kv = _rms(down[:, MLA_Q_RANK:MLA_Q_RANK + MLA_KV_RANK], gkl_ref[...]).astype(BF16)
    kr = down[:, MLA_Q_RANK + MLA_KV_RANK:]
    qf = jnp.dot(c_q, wuq_ref[...], preferred_element_type=F32)
    kf = jnp.dot(c_kv, wuk_ref[...], preferred_element_type=F32)
    v_ref[...] = (jnp.dot(c_kv, wuv_ref[...], preferred_element_type=F32) + vone_ref[...]).astype(v_ref.dtype)

    ang = pos_ref[...] * freq_ref[...]
    cos = jnp.cos(ang)
    sin = jnp.sin(ang)

    def head(t, g):
        tg = t * g
        res = jnp.dot(jnp.concatenate([tg.astype(BF16), (t * t).astype(BF16)], axis=1), rotw_ref[...],
                      preferred_element_type=F32)
        r = lax.rsqrt(res[:, LANES:] * (1.0 / MLA_QK) + NORM_EPS)
        return r * (tg * cos + res[:, :LANES] * sin)

    for h in range(MLA_HEADS):
        sl = slice(h * LANES, (h + 1) * LANES)
        q_ref[:, sl] = head(qf[:, sl], gqn_ref[...]).astype(q_ref.dtype)
        k_ref[:, sl] = head(kf[:, sl] + kr, gkn_ref[...]).astype(k_ref.dtype)


def _mla_proj(x2, pos2, g_in, w, tm):
    n, d = x2.shape
    hq = MLA_HEADS * LANES
    row = lambda c: pl.BlockSpec((tm, c), lambda i: (i, 0))
    return pl.pallas_call(
        _mla_proj_kernel,
        grid=(n // tm,),
        in_specs=[row(d), row(1), _full((1, d)), _full(w["wd"].shape), _full((1, MLA_Q_RANK)),
                  _full(w["wuq"].shape), _full((1, MLA_KV_RANK)), _full(w["wuk"].shape),
                  _full(w["wuv"].shape), _full((1, LANES)), _full((1, LANES)), _full((1, LANES)),
                  _full(w["rotw"].shape), _full((1, hq))],
        out_specs=[row(hq), row(hq), row(hq)],
        out_shape=[jax.ShapeDtypeStruct((n, hq), BF16)] * 3,
        compiler_params=_params(("parallel",)),
        name="mla_proj",
    )(x2, pos2, g_in, w["wd"], w["gql"], w["wuq"], w["gkl"], w["wuk"], w["wuv"], w["gqn"], w["gkn"],
      w["freq"], w["rotw"], w["vone"])


def _attn_kernel(qi_ref, ki_ref, q_ref, k_ref, v_ref, o_ref, m_scr, acc_scr):
    p_id = pl.program_id(2)
    qi = qi_ref[p_id]
    ki = ki_ref[p_id]
    tq = q_ref.shape[0]
    tk = k_ref.shape[0]
    rq = min(ATTN_ROW_CHUNK, tq)

    @pl.when(ki == 0)
    def _():
        m_scr[...] = jnp.full(m_scr.shape, NEG_INF, F32)
        acc_scr[...] = jnp.zeros(acc_scr.shape, F32)

    def step(diagonal):
        for hh in range(2):
            sl = slice(hh * LANES, (hh + 1) * LANES)
            for r in range(tq // rq):
                rows = slice(r * rq, (r + 1) * rq)
                nk = (r + 1) * rq if diagonal else tk
                s = lax.dot_general(q_ref[rows, sl], k_ref[:nk, sl], (((1,), (1,)), ((), ())),
                                    preferred_element_type=F32)
                if diagonal:
                    row = lax.broadcasted_iota(jnp.int32, (rq, rq), 0)
                    col = lax.broadcasted_iota(jnp.int32, (rq, rq), 1)
                    tail = jnp.where(col <= row, s[:, r * rq:], NEG_INF)
                    s = tail if r == 0 else jnp.concatenate([s[:, :r * rq], tail], axis=1)
                m_prev = m_scr[hh, rows]
                m_new = jnp.maximum(m_prev, jnp.max(s, axis=-1, keepdims=True))
                alpha = jnp.exp2(m_prev - m_new)
                p = jnp.concatenate(
                    [jnp.exp2(s[:, c * LANES:(c + 1) * LANES] - m_new).astype(BF16)
                     for c in range(nk // LANES)], axis=1)
                pv = jnp.dot(p, v_ref[:nk, sl], preferred_element_type=F32)
                acc_scr[hh, rows] = alpha * acc_scr[hh, rows] + pv
                m_scr[hh, rows] = m_new

    @pl.when(ki < qi)
    def _():
        step(False)

    @pl.when(ki == qi)
    def _():
        step(True)
        lane = lax.broadcasted_iota(jnp.int32, (tq, LANES), 1)
        a0 = acc_scr[0]
        a1 = acc_scr[1]
        o0 = a0 / pltpu.roll(a0, MLA_V, 1)
        o1 = pltpu.roll(a1, MLA_V, 1) / a1
        o_ref[...] = jnp.where(lane < MLA_V, o0, o1).astype(o_ref.dtype)


def _attention(q, k, v, batch, seq, tq):
    nq = seq // tq
    qi_tab = np.array([qi for qi in range(nq) for _ in range(qi + 1)], np.int32)
    ki_tab = np.array([ki for qi in range(nq) for ki in range(qi + 1)], np.int32)
    npairs = len(qi_tab)
    hp = MLA_HEADS // 2
    blk = lambda tab: pl.BlockSpec((tq, 2 * LANES), lambda b, h, p, qt, kt: (b * nq + (qt, kt)[tab][p], h))
    grid_spec = pltpu.PrefetchScalarGridSpec(
        num_scalar_prefetch=2,
        grid=(batch, hp, npairs),
        in_specs=[blk(0), blk(1), blk(1)],
        out_specs=pl.BlockSpec((tq, 2 * MLA_V), lambda b, h, p, qt, kt: (b * nq + qt[p], h)),
        scratch_shapes=[pltpu.VMEM((2, tq, LANES), F32), pltpu.VMEM((2, tq, LANES), F32)],
    )
    return pl.pallas_call(
        _attn_kernel,
        grid_spec=grid_spec,
        out_shape=jax.ShapeDtypeStruct((batch * seq, MLA_HEADS * MLA_V), BF16),
        compiler_params=_params(("parallel", "parallel", "arbitrary")),
        name="mla_attention",
    )(jnp.asarray(qi_tab), jnp.asarray(ki_tab), q, k, v)


def _sorted_top(cur, n, val_scr, idx_scr=None):
    rows = lax.broadcasted_iota(jnp.int32, cur.shape, 0).astype(F32)
    for r in range(n):
        m = jnp.max(cur, axis=0, keepdims=True)
        val_scr[r:r + 1, :] = m
        if idx_scr is None:
            cur = jnp.where(cur >= m, NEG_INF, cur)
        else:
            first = jnp.min(jnp.where(cur == m, rows, float(cur.shape[0])), axis=0, keepdims=True)
            idx_scr[r:r + 1, :] = first
            cur = jnp.where(rows == first, NEG_INF, cur)


def _peer_route_kernel(res_ref, a_ref, wo_ref, g_ref, wq_ref, k0_ref, k1_ref,
                       h_ref, x_ref, idx_ref, a2_ref, th_ref, s1_ref, b2_ref,
                       a_scr, b_scr, i_scr, idx_all, a2_all):
    tr = res_ref.shape[0]
    h = res_ref[...] + jnp.dot(a_ref[...], wo_ref[...], preferred_element_type=F32)
    h_ref[...] = h
    hn = _rms(h, g_ref[...])
    x_ref[...] = (hn * (1.0 / math.sqrt(2.0))).astype(x_ref.dtype)
    q = jnp.dot(_split3(hn), wq_ref[...], preferred_element_type=F32)
    q3 = [_split3(q[:, hd * LANES:(hd + 1) * LANES]) for hd in range(PEER_HEADS)]
    nk = PEER_TOPK + 1
    rows = lax.broadcasted_iota(jnp.int32, (8, tr), 0)
    a_scr[...] = jnp.full(a_scr.shape, NEG_INF, F32)
    b_scr[...] = jnp.full(b_scr.shape, NEG_INF, F32)
    contract_last = (((1,), (1,)), ((), ()))
    log2e = math.log2(math.e)
    for hd in range(PEER_HEADS):
        s0 = lax.dot_general(k0_ref[...], q3[hd], contract_last, preferred_element_type=F32)
        s1 = lax.dot_general(k1_ref[...], q3[hd], contract_last, preferred_element_type=F32)
        _sorted_top(s0, nk, a_scr, i_scr)
        _sorted_top(s1, nk, b_scr)
        pieces = [b_scr[...] + a_scr[0:1, :]]
        r = 1
        while nk // (r + 1) >= 2:
            pieces.append(jnp.where(rows < nk // (r + 1), b_scr[0:8, :] + a_scr[r:r + 1, :], NEG_INF))
            r += 1
        assert r == 8
        pieces.append(a_scr[8:24, :] + b_scr[0:1, :])
        cand = jnp.concatenate(pieces, axis=0)
        cur = cand
        best = jnp.max(cur, axis=0, keepdims=True)
        for _ in range(PEER_TOPK - 1):
            m = jnp.max(cur, axis=0, keepdims=True)
            cur = jnp.where(cur >= m, NEG_INF, cur)
        v16 = jnp.max(cur, axis=0, keepdims=True)
        cur = jnp.where(cur >= v16, NEG_INF, cur)
        v17 = jnp.max(cur, axis=0, keepdims=True)
        tau = 0.5 * (v16 + v17)
        z = jnp.sum(jnp.where(cand >= tau, jnp.exp(cand - best), 0.0), axis=0, keepdims=True)
        a_max = a_scr[0:1, :]
        a_rel = (a_scr[0:PEER_TOPK, :] - a_max) * log2e
        s1_rel = (s1 - (best - a_max + jnp.log(z) + 0.5 * math.log(2.0))) * log2e
        thr = (tau - (best + jnp.log(z) + 0.5 * math.log(2.0))) * log2e
        slots = slice(hd * PEER_TOPK, (hd + 1) * PEER_TOPK)
        idx_all[slots, :] = i_scr[0:PEER_TOPK, :]
        a2_all[slots, :] = jnp.exp2(a_rel)
        th_ref[slots, :] = thr - a_rel
        s1t = s1_rel.T
        s1_ref[hd] = s1t
        b2_ref[hd] = jnp.exp2(s1t)
    idx_ref[...] = idx_all[...].T
    a2_ref[...] = a2_all[...].T


def _peer_route(res, a, w_o, g, wq, k0, k1, tr):
    n, d = res.shape
    da = a.shape[1]
    nh = PEER_HEADS
    ns = nh * PEER_TOPK
    row = lambda c: pl.BlockSpec((tr, c), lambda i: (i, 0))
    per_head = pl.BlockSpec((nh, tr, PEER_NKEYS), lambda i: (0, i, 0))
    return pl.pallas_call(
        _peer_route_kernel,
        grid=(n // tr,),
        in_specs=[row(d), row(da), _full(w_o.shape), _full((1, d)), _full(wq.shape), _full(k0.shape),
                  _full(k1.shape)],
        out_specs=[row(d), row(d), row(ns), row(ns), pl.BlockSpec((ns, tr), lambda i: (0, i)),
                   per_head, per_head],
        out_shape=[jax.ShapeDtypeStruct((n, d), F32), jax.ShapeDtypeStruct((n, d), BF16),
                   jax.ShapeDtypeStruct((n, ns), F32), jax.ShapeDtypeStruct((n, ns), F32),
                   jax.ShapeDtypeStruct((ns, n), F32),
                   jax.ShapeDtypeStruct((nh, n, PEER_NKEYS), F32),
                   jax.ShapeDtypeStruct((nh, n, PEER_NKEYS), F32)],
        scratch_shapes=[pltpu.VMEM((24, tr), F32), pltpu.VMEM((24, tr), F32), pltpu.VMEM((24, tr), F32),
                        pltpu.VMEM((ns, tr), F32), pltpu.VMEM((ns, tr), F32)],
        compiler_params=_params(("parallel",)),
        name="peer_route",
    )(res, a, w_o, g, wq, k0, k1)


def _peer_weights_kernel(idx_ref, a2_ref, th_ref, s1_ref, b2_ref, g_ref):
    tg = idx_ref.shape[0]
    ns = th_ref.shape[0]
    sub = lax.broadcasted_iota(jnp.int32, (PEER_NKEYS, ns), 0).astype(F32)

    def head_rows(ref, t):
        return jnp.concatenate([jnp.broadcast_to(ref[hd, t:t + 1, :], (PEER_TOPK, PEER_NKEYS))
                                for hd in range(PEER_HEADS)], axis=0)

    for t in range(tg):
        first = jnp.where(sub == idx_ref[t:t + 1, :], a2_ref[t:t + 1, :], 0.0).astype(BF16)
        thr = jnp.broadcast_to(th_ref[:, t:t + 1], (ns, PEER_NKEYS))
        second = jnp.where(head_rows(s1_ref, t) >= thr, head_rows(b2_ref, t), 0.0).astype(BF16)
        gt = jnp.dot(first, second, preferred_element_type=F32)
        for blk in range(PEER_NKEYS // PEER_STEP_KEYS):
            g_ref[blk, t] = gt[blk * PEER_STEP_KEYS:(blk + 1) * PEER_STEP_KEYS, :]


def _peer_weights(idx, a2, th, s1, b2, tg):
    n, ns = idx.shape
    nh = PEER_HEADS
    nb = PEER_NKEYS // PEER_STEP_KEYS
    row = pl.BlockSpec((tg, ns), lambda i: (i, 0))
    per_head = pl.BlockSpec((nh, tg, PEER_NKEYS), lambda i: (0, i, 0))
    return pl.pallas_call(
        _peer_weights_kernel,
        grid=(n // tg,),
        in_specs=[row, row, pl.BlockSpec((ns, tg), lambda i: (0, i)), per_head, per_head],
        out_specs=pl.BlockSpec((nb, tg, PEER_STEP_KEYS, PEER_NKEYS), lambda i: (0, i, 0, 0)),
        out_shape=jax.ShapeDtypeStruct((nb, n, PEER_STEP_KEYS, PEER_NKEYS), F32),
        compiler_params=_params(("parallel",)),
        name="peer_weights",
    )(idx, a2, th, s1, b2)


def _peer_dense_kernel(x_ref, g_ref, u_ref, v_ref, h_ref, o_ref, acc_scr):
    e = pl.program_id(1)
    ne = pl.num_programs(1)
    t = x_ref.shape[0]
    rows_per_step = PEER_STEP_KEYS

    @pl.when(e == 0)
    def _():
        acc_scr[...] = jnp.zeros(acc_scr.shape, F32)

    y = lax.dot_general(x_ref[...], u_ref[...], (((1,), (1,)), ((), ())),
                        preferred_element_type=F32)
    ps = []
    for ii in range(rows_per_step):
        yc = y[:, ii * PEER_NKEYS:(ii + 1) * PEER_NKEYS]
        gi = g_ref[pl.ds(ii, t, stride=rows_per_step), :]
        ps.append((gi * (yc * (1.0 + lax.erf(yc)))).astype(BF16))
    acc_scr[...] += jnp.dot(jnp.concatenate(ps, axis=1), v_ref[...], preferred_element_type=F32)

    @pl.when(e == ne - 1)
    def _():
        o_ref[...] = h_ref[...] + acc_scr[...]


def _peer_dense(x, g, u, v, h, t, eb):
    n, d = x.shape
    ne = u.shape[0] // eb
    tok = lambda c: pl.BlockSpec((t, c), lambda i, e: (i, 0))
    return pl.pallas_call(
        _peer_dense_kernel,
        grid=(n // t, ne),
        in_specs=[tok(d),
                  pl.BlockSpec((None, t * PEER_STEP_KEYS, PEER_NKEYS), lambda i, e: (e, i, 0)),
                  pl.BlockSpec((eb, d), lambda i, e: (e, 0)),
                  pl.BlockSpec((eb, d), lambda i, e: (e, 0)),
                  pl.BlockSpec((t, d), lambda i, e: (i, 0), pipeline_mode=pl.Buffered(1))],
        out_specs=tok(d),
        out_shape=jax.ShapeDtypeStruct((n, d), F32),
        scratch_shapes=[pltpu.VMEM((t, d), F32)],
        compiler_params=_params(("parallel", "arbitrary")),
        name="peer_dense",
    )(x, g, u, v, h)


def _gla_proj_kernel(x_ref, gin_ref, win_ref, wg2_ref, bg_ref, q_ref, k_ref, v_ref, r_ref, la_ref):
    hk = GLA_HEADS * GLA_DK
    hv = GLA_HEADS * GLA_DV
    hn = _rms(x_ref[...], gin_ref[...])
    proj = jnp.dot(hn.astype(BF16), win_ref[...], preferred_element_type=F32)
    q_ref[...] = proj[:, :hk] * (GLA_DK ** -0.5)
    k_ref[...] = proj[:, hk:2 * hk]
    v_ref[...] = proj[:, 2 * hk:2 * hk + hv]
    r_ref[...] = proj[:, 2 * hk + hv:2 * hk + 2 * hv]
    g_low = proj[:, 2 * hk + 2 * hv:]
    z = jnp.dot(g_low, wg2_ref[...], precision=HIGHEST, preferred_element_type=F32) + bg_ref[...]
    la_ref[...] = (jnp.minimum(z, 0.0) - jnp.log1p(jnp.exp(-jnp.abs(z)))) * (1.0 / GLA_TAU)


def _gla_proj(x2, g_in, w, tm):
    n, d = x2.shape
    hk = GLA_HEADS * GLA_DK
    hv = GLA_HEADS * GLA_DV
    row = lambda c: pl.BlockSpec((tm, c), lambda i: (i, 0))
    return pl.pallas_call(
        _gla_proj_kernel,
        grid=(n // tm,),
        in_specs=[row(d), _full((1, d)), _full(w["win"].shape), _full(w["wg2"].shape), _full((1, hk))],
        out_specs=[row(hk), row(hk), row(hv), row(hv), row(hk)],
        out_shape=[jax.ShapeDtypeStruct((n, hk), F32), jax.ShapeDtypeStruct((n, hk), F32),
                   jax.ShapeDtypeStruct((n, hv), F32), jax.ShapeDtypeStruct((n, hv), F32),
                   jax.ShapeDtypeStruct((n, hk), F32)],
        compiler_params=_params(("parallel",)),
        name="gla_proj",
    )(x2, g_in, w["win"], w["wg2"], w["bg"])


def _gla_chunk_kernel(q_ref, k_ref, v_ref, r_ref, la_ref, gon_ref, o_ref, state_scr):
    c_len = GLA_CHUNK
    n_chunks = q_ref.shape[0] // c_len

    @pl.when(pl.program_id(1) == 0)
    def _():
        state_scr[...] = jnp.zeros(state_scr.shape, F32)

    ri = lax.broadcasted_iota(jnp.int32, (c_len, c_len), 0)
    ci = lax.broadcasted_iota(jnp.int32, (c_len, c_len), 1)
    causal = ci <= ri
    tril = causal.astype(F32)
    contract_last = (((1,), (1,)), ((), ()))

    def chunk(c, carry):
        r0 = pl.multiple_of(c * c_len, c_len)
        rows = pl.ds(r0, c_len)
        for h in range(GLA_HEADS):
            ksl = slice(h * GLA_DK, (h + 1) * GLA_DK)
            vsl = slice(h * GLA_DV, (h + 1) * GLA_DV)
            qc = q_ref[rows, ksl]
            kc = k_ref[rows, ksl]
            vc = v_ref[rows, vsl].astype(BF16)
            b = jnp.dot(tril, la_ref[rows, ksl], precision=HIGHEST, preferred_element_type=F32)
            b_last = b[c_len - 1:c_len, :]
            q_dec = (qc * jnp.exp(b)).astype(BF16)
            k_inv = (kc * jnp.exp(-b)).astype(BF16)
            k_rem = kc * jnp.exp(b_last - b)
            attn = lax.dot_general(q_dec, k_inv, contract_last, preferred_element_type=F32)
            attn = jnp.where(causal, attn, 0.0).astype(BF16)
            state = state_scr[h]
            o = (jnp.dot(attn, vc, preferred_element_type=F32)
                 + jnp.dot(q_dec, state.astype(BF16), preferred_element_type=F32))
            kv = jnp.dot(k_rem.T.astype(BF16), vc, preferred_element_type=F32)
            dec_col = jnp.broadcast_to(jnp.exp(b_last), (8, GLA_DK)).T[:, 0:1]
            state_scr[h] = state * dec_col + kv
            on = _rms(o, gon_ref[...])
            rr = r_ref[rows, vsl]
            o_ref[rows, vsl] = (on * (rr * jax.nn.sigmoid(rr))).astype(o_ref.dtype)
        return carry

    lax.fori_loop(0, n_chunks, chunk, 0, unroll=GLA_UNROLL)


def _gla_chunk(q, k, v, r, la, g_on, batch, seq, tc):
    hk = GLA_HEADS * GLA_DK
    hv = GLA_HEADS * GLA_DV
    ns = seq // tc
    row = lambda c: pl.BlockSpec((tc, c), lambda b, s: (b * ns + s, 0))
    return pl.pallas_call(
        _gla_chunk_kernel,
        grid=(batch, ns),
        in_specs=[row(hk), row(hk), row(hv), row(hv), row(hk), _full((1, GLA_DV))],
        out_specs=row(hv),
        out_shape=jax.ShapeDtypeStruct((batch * seq, hv), BF16),
        scratch_shapes=[pltpu.VMEM((GLA_HEADS, GLA_DK, GLA_DV), F32)],
        compiler_params=_params(("parallel", "arbitrary")),
        name="gla_chunk",
    )(q, k, v, r, la, g_on)


def _prep_mla(w_down, g_q_lat, w_uq, g_kv_lat, w_ukv, g_qn, g_kn):
    d = w_down.shape[0]
    lat = MLA_Q_RANK + MLA_KV_RANK
    z = lambda c: jnp.zeros((d, c), F32)
    wd = jnp.concatenate([w_down[:, :lat], z(MLA_NOPE), w_down[:, lat:], z(LANES - MLA_QK)], axis=1)
    pad_head = LANES - MLA_QK
    wuq = jnp.pad(w_uq.reshape(MLA_Q_RANK, MLA_HEADS, MLA_QK), ((0, 0), (0, 0), (0, pad_head)))
    wukv = w_ukv.reshape(MLA_KV_RANK, MLA_HEADS, MLA_NOPE + MLA_V)
    wuk = jnp.pad(wukv[:, :, :MLA_NOPE], ((0, 0), (0, 0), (0, LANES - MLA_NOPE)))
    wuv = jnp.pad(wukv[:, :, MLA_NOPE:], ((0, 0), (0, 0), (0, LANES - MLA_V)))
    half = MLA_ROPE // 2
    inv_freq = ROPE_THETA ** (-jnp.arange(half, dtype=F32) / half)
    zero = lambda c: jnp.zeros((c,), F32)
    vone = jnp.tile(jnp.concatenate([zero(MLA_V), jnp.ones((LANES - MLA_V,), F32)]), MLA_HEADS)
    freq = jnp.concatenate([zero(MLA_NOPE), inv_freq, inv_freq, zero(pad_head)])
    rotw = np.zeros((2 * LANES, 2 * LANES), np.float32)
    for lane in range(MLA_NOPE, MLA_NOPE + half):
        rotw[lane + half, lane] = -1.0
        rotw[lane, lane + half] = 1.0
    rotw[LANES:, LANES:] = 1.0
    scale = MLA_QK ** -0.5 * math.log2(math.e)
    return {
        "wd": wd.astype(BF16),
        "gql": g_q_lat.reshape(1, -1), "gkl": g_kv_lat.reshape(1, -1),
        "wuq": wuq.reshape(MLA_Q_RANK, MLA_HEADS * LANES).astype(BF16),
        "wuk": wuk.reshape(MLA_KV_RANK, MLA_HEADS * LANES).astype(BF16),
        "wuv": wuv.reshape(MLA_KV_RANK, MLA_HEADS * LANES).astype(BF16),
        "vone": vone.reshape(1, MLA_HEADS * LANES),
        "gqn": (jnp.pad(g_qn, (0, pad_head)) * scale).reshape(1, LANES),
        "gkn": jnp.pad(g_kn, (0, pad_head)).reshape(1, LANES),
        "freq": freq.reshape(1, LANES), "rotw": jnp.asarray(rotw, BF16),
    }


def _prep_gla(w_in, w_g2, b_g):
    pad = LANES - GLA_GATE_RANK
    return {
        "win": jnp.pad(w_in, ((0, 0), (0, pad))).astype(BF16),
        "wg2": jnp.pad(w_g2, ((0, pad), (0, 0))),
        "bg": b_g.reshape(1, -1),
    }


def _prep_peer(w_query, sub_keys, u_tab, v_tab):
    zk = jnp.zeros((PEER_NKEYS, PEER_HALF), F32)
    return {
        "wq": _split3_const(w_query, 0),
        "k0": _split3_const(jnp.concatenate([sub_keys[0], zk], axis=1), 1),
        "k1": _split3_const(jnp.concatenate([zk, sub_keys[1]], axis=1), 1),
        "u": u_tab.astype(BF16),
        "v": v_tab.astype(BF16),
    }


def _peer_layer(res, a, w_o, g, pw, tiles):
    h, xn, idx, a2, th, s1, b2 = _peer_route(res, a, w_o, g.reshape(1, -1), pw["wq"], pw["k0"], pw["k1"],
                                             tiles["route"])
    gate = _peer_weights(idx, a2, th, s1, b2, tiles["weights"])
    gate = gate.reshape(gate.shape[0], -1, PEER_NKEYS)
    return _peer_dense(xn, gate, pw["u"], pw["v"], h, tiles["dense_t"], PEER_STEP_KEYS * PEER_NKEYS)


def _tiles(seq):
    pick = lambda want: math.gcd(want, seq)
    return {"proj": pick(512), "attn": pick(2048), "route": pick(512), "weights": pick(128),
            "dense_t": pick(1024), "gla": pick(512)}


def kernel(x, positions, attn_norm_g, ffn_norm_g, mla_w_down, mla_g_q_lat, mla_w_uq, mla_g_kv_lat,
           mla_w_ukv, mla_g_qn, mla_g_kn, mla_w_o, gla_w_in, gla_w_g2, gla_b_g, gla_g_on, gla_w_o,
           peer_w_query, peer_sub_keys, peer_u, peer_v):
    batch, seq, d = x.shape
    n = batch * seq
    depth = attn_norm_g.shape[0]
    tiles = _tiles(seq)
    h = x.reshape(n, d)
    pos = positions.reshape(n, 1).astype(F32)
    for i in range(depth):
        j = i // 2
        g_in = attn_norm_g[i].reshape(1, d)
        if i % 2 == 0:
            w = _prep_mla(mla_w_down[j], mla_g_q_lat[j], mla_w_uq[j], mla_g_kv_lat[j], mla_w_ukv[j],
                          mla_g_qn[j], mla_g_kn[j])
            q, k, v = _mla_proj(h, pos, g_in, w, tiles["proj"])
            a = _attention(q, k, v, batch, seq, tiles["attn"])
            w_o = mla_w_o[j].astype(BF16)
        else:
            w = _prep_gla(gla_w_in[j], gla_w_g2[j], gla_b_g[j])
            q, k, v, r, la = _gla_proj(h, g_in, w, tiles["proj"])
            a = _gla_chunk(q, k, v, r, la, gla_g_on[j].reshape(1, -1), batch, seq, tiles["gla"])
            w_o = gla_w_o[j].astype(BF16)
        pw = _prep_peer(peer_w_query[i], peer_sub_keys[i], peer_u[i], peer_v[i])
        h = _peer_layer(h, a, w_o, ffn_norm_g[i], pw, tiles)
    return h.reshape(batch, seq, d)
```

```python
import math

import jax
import jax.numpy as jnp
import numpy as np
from jax import lax
from jax.experimental import pallas as pl
from jax.experimental.pallas import tpu as pltpu

F32 = jnp.float32
BF16 = jnp.bfloat16
HIGHEST = lax.Precision.HIGHEST

LANES = 128
NORM_EPS = 1e-6
VMEM_LIMIT = 56 * 1024 * 1024

MLA_HEADS = 16
MLA_Q_RANK = 384
MLA_KV_RANK = 256
MLA_NOPE = 64
MLA_ROPE = 32
MLA_QK = MLA_NOPE + MLA_ROPE
MLA_V = 64
ROPE_THETA = 10000.0
GLA_HEADS = 4
GLA_DK = 128
GLA_DV = 256
GLA_GATE_RANK = 16
GLA_TAU = 16.0
GLA_CHUNK = 64
PEER_HEADS = 8
PEER_NKEYS = 128
PEER_HALF = 64
PEER_TOPK = 16
NEG_INF = float("-inf")
PEER_STEP_KEYS = 8
ATTN_ROW_CHUNK = 1024


def _params(sem):
    return pltpu.CompilerParams(dimension_semantics=sem, vmem_limit_bytes=VMEM_LIMIT)


def _rms(x, g, n=None):
    n = x.shape[-1] if n is None else n
    ss = jnp.sum(x * x, axis=-1, keepdims=True)
    return x * lax.rsqrt(ss * (1.0 / n) + NORM_EPS) * g


def _split3(x):
    hi = x.astype(BF16)
    lo = (x - hi.astype(F32)).astype(BF16)
    return jnp.concatenate([hi, lo, hi], axis=-1)


def _split3_const(w, axis):
    bits = lax.bitcast_convert_type(w, jnp.uint32) & jnp.uint32(0xFFFF0000)
    hi = lax.bitcast_convert_type(bits, F32)
    lo = (w - hi).astype(BF16)
    hi = hi.astype(BF16)
    return jnp.concatenate([hi, hi, lo], axis=axis)


def _full(shape):
    nd = len(shape)
    return pl.BlockSpec(shape, lambda *_: (0,) * nd)


def _mla_proj_kernel(x_ref, pos_ref, gin_ref, wd_ref, gql_ref, wuq_ref, gkl_ref, wuk_ref, wuv_ref,
                     gqn_ref, gkn_ref, freq_ref, rotw_ref, vone_ref, q_ref, k_ref, v_ref):
    x = x_ref[...]
    hn = _rms(x, gin_ref[...])
    down = jnp.dot(hn.astype(BF16), wd_ref[...], preferred_element_type=F32)
    c_q = _rms(down[:, :MLA_Q_RANK], gql_ref[...]).astype(BF16)
    c_kv = _rms(down[:, MLA_Q_RANK:MLA_Q_RANK + MLA_KV_RANK], gkl_ref[...]).astype(BF16)
    kr = down[:, MLA_Q_RANK + MLA_KV_RANK:]
    qf = jnp.dot(c_q, wuq_ref[...], preferred_element_type=F32)
    kf = jnp.dot(c_kv, wuk_ref[...], preferred_element_type=F32)
    v_ref[...] = (jnp.dot(c_kv, wuv_ref[...], preferred_element_type=F32) + vone_ref[...]).astype(v_ref.dtype)

    ang = pos_ref[...] * freq_ref[...]
    cos = jnp.cos(ang)
    sin = jnp.sin(ang)

    def head(t, g):
        tg = t * g
        res = jnp.dot(jnp.concatenate([tg.astype(BF16), (t * t).astype(BF16)], axis=1), rotw_ref[...],
                      preferred_element_type=F32)
        r = lax.rsqrt(res[:, LANES:] * (1.0 / MLA_QK) + NORM_EPS)
        return r * (tg * cos + res[:, :LANES] * sin)

    for h in range(MLA_HEADS):
        sl = slice(h * LANES, (h + 1) * LANES)
        q_ref[:, sl] = head(qf[:, sl], gqn_ref[...]).astype(q_ref.dtype)
        k_ref[:, sl] = head(kf[:, sl] + kr, gkn_ref[...]).astype(k_ref.dtype)


def _mla_proj(x2, pos2, g_in, w, tm):
    n, d = x2.shape
    hq = MLA_HEADS * LANES
    row = lambda c: pl.BlockSpec((tm, c), lambda i: (i, 0))
    return pl.pallas_call(
        _mla_proj_kernel,
        grid=(n // tm,),
        in_specs=[row(d), row(1), _full((1, d)), _full(w["wd"].shape), _full((1, MLA_Q_RANK)),
                  _full(w["wuq"].shape), _full((1, MLA_KV_RANK)), _full(w["wuk"].shape),
                  _full(w["wuv"].shape), _full((1, LANES)), _full((1, LANES)), _full((1, LANES)),
                  _full(w["rotw"].shape), _full((1, hq))],
        out_specs=[row(hq), row(hq), row(hq)],
        out_shape=[jax.ShapeDtypeStruct((n, hq), BF16)] * 3,
        compiler_params=_params(("parallel",)),
        name="mla_proj",
    )(x2, pos2, g_in, w["wd"], w["gql"], w["wuq"], w["gkl"], w["wuk"], w["wuv"], w["gqn"], w["gkn"],
      w["freq"], w["rotw"], w["vone"])


def _attn_kernel(qi_ref, ki_ref, q_ref, k_ref, v_ref, o_ref, m_scr, acc_scr):
    p_id = pl.program_id(2)
    qi = qi_ref[p_id]
    ki = ki_ref[p_id]
    tq = q_ref.shape[0]
    tk = k_ref.shape[0]
    rq = min(ATTN_ROW_CHUNK, tq)

    @pl.when(ki == 0)
    def _():
        m_scr[...] = jnp.full(m_scr.shape, NEG_INF, F32)
        acc_scr[...] = jnp.zeros(acc_scr.shape, F32)

    def step(diagonal):
        for hh in range(2):
            sl = slice(hh * LANES, (hh + 1) * LANES)
            for r in range(tq // rq):
                rows = slice(r * rq, (r + 1) * rq)
                nk = (r + 1) * rq if diagonal else tk
                s = lax.dot_general(q_ref[rows, sl], k_ref[:nk, sl], (((1,), (1,)), ((), ())),
                                    preferred_element_type=F32)
                if diagonal:
                    row = lax.broadcasted_iota(jnp.int32, (rq, rq), 0)
                    col = lax.broadcasted_iota(jnp.int32, (rq, rq), 1)
                    tail = jnp.where(col <= row, s[:, r * rq:], NEG_INF)
                    s = tail if r == 0 else jnp.concatenate([s[:, :r * rq], tail], axis=1)
                m_prev = m_scr[hh, rows]
                m_new = jnp.maximum(m_prev, jnp.max(s, axis=-1, keepdims=True))
                alpha = jnp.exp2(m_prev - m_new)
                p = jnp.concatenate(
                    [jnp.exp2(s[:, c * LANES:(c + 1) * LANES] - m_new).astype(BF16)
                     for c in range(nk // LANES)], axis=1)
                pv = jnp.dot(p, v_ref[:nk, sl], preferred_element_type=F32)
                acc_scr[hh, rows] = alpha * acc_scr[hh, rows] + pv
                m_scr[hh, rows] = m_new

    @pl.when(ki < qi)
    def _():
        step(False)

    @pl.when(ki == qi)
    def _():
        step(True)
        lane = lax.broadcasted_iota(jnp.int32, (tq, LANES), 1)
        a0 = acc_scr[0]
        a1 = acc_scr[1]
        o0 = a0 / pltpu.roll(a0, MLA_V, 1)
        o1 = pltpu.roll(a1, MLA_V, 1) / a1
        o_ref[...] = jnp.where(lane < MLA_V, o0, o1).astype(o_ref.dtype)


def _attention(q, k, v, batch, seq, tq):
    nq = seq // tq
    qi_tab = np.array([qi for qi in range(nq) for _ in range(qi + 1)], np.int32)
    ki_tab = np.array([ki for qi in range(nq) for ki in range(qi + 1)], np.int32)
    npairs = len(qi_tab)
    hp = MLA_HEADS // 2
    blk = lambda tab: pl.BlockSpec((tq, 2 * LANES), lambda b, h, p, qt, kt: (b * nq + (qt, kt)[tab][p], h))
    grid_spec = pltpu.PrefetchScalarGridSpec(
        num_scalar_prefetch=2,
        grid=(batch, hp, npairs),
        in_specs=[blk(0), blk(1), blk(1)],
        out_specs=pl.BlockSpec((tq, 2 * MLA_V), lambda b, h, p, qt, kt: (b * nq + qt[p], h)),
        scratch_shapes=[pltpu.VMEM((2, tq, LANES), F32), pltpu.VMEM((2, tq, LANES), F32)],
    )
    return pl.pallas_call(
        _attn_kernel,
        grid_spec=grid_spec,
        out_shape=jax.ShapeDtypeStruct((batch * seq, MLA_HEADS * MLA_V), BF16),
        compiler_params=_params(("parallel", "parallel", "arbitrary")),
        name="mla_attention",
    )(jnp.asarray(qi_tab), jnp.asarray(ki_tab), q, k, v)


def _sort_network(n):
    pairs = []
    p = 1
    while p < n:
        k = p
        while k >= 1:
            for j in range(k % p, n - k, 2 * k):
                for i in range(min(k, n - j - k)):
                    if (i + j) // (2 * p) == (i + j + k) // (2 * p):
                        pairs.append((i + j, i + j + k))
            k //= 2
        p *= 2
    return pairs


def _sorted_top(cur, n, val_scr, idx_scr=None):
    sub = 8
    groups = cur.shape[0] // sub
    track = idx_scr is not None
    base = lax.broadcasted_iota(jnp.int32, (sub, LANES), 0).astype(F32)
    network = _sort_network(groups)
    for c in range(cur.shape[1] // LANES):
        csl = slice(c * LANES, (c + 1) * LANES)
        vals = [cur[g * sub:(g + 1) * sub, csl] for g in range(groups)]
        if track:
            idxs = [base + float(g * sub) for g in range(groups)]
        for a, b in network:
            if track:
                keep = vals[a] >= vals[b]
                idxs[a], idxs[b] = jnp.where(keep, idxs[a], idxs[b]), jnp.where(keep, idxs[b], idxs[a])
            vals[a], vals[b] = jnp.maximum(vals[a], vals[b]), jnp.minimum(vals[a], vals[b])
        for r in range(n):
            m = jnp.max(vals[0], axis=0, keepdims=True)
            val_scr[r:r + 1, csl] = m
            hit = vals[0] == m
            if track:
                first = jnp.min(jnp.where(hit, base, float(sub)), axis=0, keepdims=True)
                hit = base == first
                idx_scr[r:r + 1, csl] = jnp.max(jnp.where(hit, idxs[0], -1.0), axis=0, keepdims=True)
            depth = min(n - 1 - r, groups)
            for k in range(depth):
                below = vals[k + 1] if k + 1 < groups else NEG_INF
                vals[k] = jnp.where(hit, below, vals[k])
                if track and k + 1 < groups:
                    idxs[k] = jnp.where(hit, idxs[k + 1], idxs[k])


def _peer_route_kernel(res_ref, a_ref, wo_ref, g_ref, wq_ref, k0_ref, k1_ref,
                       h_ref, x_ref, idx_ref, a2_ref, th_ref, s1_ref, b2_ref,
                       a_scr, b_scr, i_scr, idx_all, a2_all):
    tr = res_ref.shape[0]
    h = res_ref[...] + jnp.dot(a_ref[...], wo_ref[...], preferred_element_type=F32)
    h_ref[...] = h
    hn = _rms(h, g_ref[...])
    x_ref[...] = (hn * (1.0 / math.sqrt(2.0))).astype(x_ref.dtype)
    q = jnp.dot(_split3(hn), wq_ref[...], preferred_element_type=F32)
    q3 = [_split3(q[:, hd * LANES:(hd + 1) * LANES]) for hd in range(PEER_HEADS)]
    nk = PEER_TOPK + 1
    rows = lax.broadcasted_iota(jnp.int32, (8, tr), 0)
    a_scr[...] = jnp.full(a_scr.shape, NEG_INF, F32)
    b_scr[...] = jnp.full(b_scr.shape, NEG_INF, F32)
    contract_last = (((1,), (1,)), ((), ()))
    log2e = math.log2(math.e)
    for hd in range(PEER_HEADS):
        s0 = lax.dot_general(k0_ref[...], q3[hd], contract_last, preferred_element_type=F32)
        s1 = lax.dot_general(k1_ref[...], q3[hd], contract_last, preferred_element_type=F32)
        _sorted_top(s0, nk, a_scr, i_scr)
        _sorted_top(s1, nk, b_scr)
        pieces = [b_scr[...] + a_scr[0:1, :]]
        r = 1
        while nk // (r + 1) >= 2:
            pieces.append(jnp.where(rows < nk // (r + 1), b_scr[0:8, :] + a_scr[r:r + 1, :], NEG_INF))
            r += 1
        assert r == 8
        pieces.append(a_scr[8:24, :] + b_scr[0:1, :])
        cand = jnp.concatenate(pieces, axis=0)
        cur = cand
        best = jnp.max(cur, axis=0, keepdims=True)
        for _ in range(PEER_TOPK - 1):
            m = jnp.max(cur, axis=0, keepdims=True)
            cur = jnp.where(cur >= m, NEG_INF, cur)
        v16 = jnp.max(cur, axis=0, keepdims=True)
        cur = jnp.where(cur >= v16, NEG_INF, cur)
        v17 = jnp.max(cur, axis=0, keepdims=True)
        tau = 0.5 * (v16 + v17)
        z = jnp.sum(jnp.where(cand >= tau, jnp.exp(cand - best), 0.0), axis=0, keepdims=True)
        a_max = a_scr[0:1, :]
        a_rel = (a_scr[0:PEER_TOPK, :] - a_max) * log2e
        s1_rel = (s1 - (best - a_max + jnp.log(z) + 0.5 * math.log(2.0))) * log2e
        thr = (tau - (best + jnp.log(z) + 0.5 * math.log(2.0))) * log2e
        slots = slice(hd * PEER_TOPK, (hd + 1) * PEER_TOPK)
        idx_all[slots, :] = i_scr[0:PEER_TOPK, :]
        a2_all[slots, :] = jnp.exp2(a_rel)
        th_ref[slots, :] = thr - a_rel
        s1t = s1_rel.T
        s1_ref[hd] = s1t
        b2_ref[hd] = jnp.exp2(s1t)
    idx_ref[...] = idx_all[...].T
    a2_ref[...] = a2_all[...].T


def _peer_route(res, a, w_o, g, wq, k0, k1, tr):
    n, d = res.shape
    da = a.shape[1]
    nh = PEER_HEADS
    ns = nh * PEER_TOPK
    row = lambda c: pl.BlockSpec((tr, c), lambda i: (i, 0))
    per_head = pl.BlockSpec((nh, tr, PEER_NKEYS), lambda i: (0, i, 0))
    return pl.pallas_call(
        _peer_route_kernel,
        grid=(n // tr,),
        in_specs=[row(d), row(da), _full(w_o.shape), _full((1, d)), _full(wq.shape), _full(k0.shape),
                  _full(k1.shape)],
        out_specs=[row(d), row(d), row(ns), row(ns), pl.BlockSpec((ns, tr), lambda i: (0, i)),
                   per_head, per_head],
        out_shape=[jax.ShapeDtypeStruct((n, d), F32), jax.ShapeDtypeStruct((n, d), BF16),
                   jax.ShapeDtypeStruct((n, ns), F32), jax.ShapeDtypeStruct((n, ns), F32),
                   jax.ShapeDtypeStruct((ns, n), F32),
                   jax.ShapeDtypeStruct((nh, n, PEER_NKEYS), F32),
                   jax.ShapeDtypeStruct((nh, n, PEER_NKEYS), F32)],
        scratch_shapes=[pltpu.VMEM((24, tr), F32), pltpu.VMEM((24, tr), F32), pltpu.VMEM((24, tr), F32),
                        pltpu.VMEM((ns, tr), F32), pltpu.VMEM((ns, tr), F32)],
        compiler_params=_params(("parallel",)),
        name="peer_route",
    )(res, a, w_o, g, wq, k0, k1)


def _peer_weights_kernel(idx_ref, a2_ref, th_ref, s1_ref, b2_ref, g_ref):
    tg = idx_ref.shape[0]
    ns = th_ref.shape[0]
    sub = lax.broadcasted_iota(jnp.int32, (PEER_NKEYS, ns), 0).astype(F32)

    def head_rows(ref, t):
        return jnp.concatenate([jnp.broadcast_to(ref[hd, t:t + 1, :], (PEER_TOPK, PEER_NKEYS))
                                for hd in range(PEER_HEADS)], axis=0)

    for t in range(tg):
        first = jnp.where(sub == idx_ref[t:t + 1, :], a2_ref[t:t + 1, :], 0.0).astype(BF16)
        thr = jnp.broadcast_to(th_ref[:, t:t + 1], (ns, PEER_NKEYS))
        second = jnp.where(head_rows(s1_ref, t) >= thr, head_rows(b2_ref, t), 0.0).astype(BF16)
        gt = jnp.dot(first, second, preferred_element_type=F32)
        for blk in range(PEER_NKEYS // PEER_STEP_KEYS):
            g_ref[blk, t] = gt[blk * PEER_STEP_KEYS:(blk + 1) * PEER_STEP_KEYS, :]


def _peer_weights(idx, a2, th, s1, b2, tg):
    n, ns = idx.shape
    nh = PEER_HEADS
    nb = PEER_NKEYS // PEER_STEP_KEYS
    row = pl.BlockSpec((tg, ns), lambda i: (i, 0))
    per_head = pl.BlockSpec((nh, tg, PEER_NKEYS), lambda i: (0, i, 0))
    return pl.pallas_call(
        _peer_weights_kernel,
        grid=(n // tg,),
        in_specs=[row, row, pl.BlockSpec((ns, tg), lambda i: (0, i)), per_head, per_head],
        out_specs=pl.BlockSpec((nb, tg, PEER_STEP_KEYS, PEER_NKEYS), lambda i: (0, i, 0, 0)),
        out_shape=jax.ShapeDtypeStruct((nb, n, PEER_STEP_KEYS, PEER_NKEYS), F32),
        compiler_params=_params(("parallel",)),
        name="peer_weights",
    )(idx, a2, th, s1, b2)


def _peer_dense_kernel(x_ref, g_ref, u_ref, v_ref, h_ref, o_ref, acc_scr):
    e = pl.program_id(1)
    ne = pl.num_programs(1)
    t = x_ref.shape[0]
    rows_per_step = PEER_STEP_KEYS

    @pl.when(e == 0)
    def _():
        acc_scr[...] = jnp.zeros(acc_scr.shape, F32)

    y = lax.dot_general(x_ref[...], u_ref[...], (((1,), (1,)), ((), ())),
                        preferred_element_type=F32)
    ps = []
    for ii in range(rows_per_step):
        yc = y[:, ii * PEER_NKEYS:(ii + 1) * PEER_NKEYS]
        gi = g_ref[pl.ds(ii, t, stride=rows_per_step), :]
        ps.append((gi * (yc * (1.0 + lax.erf(yc)))).astype(BF16))
    acc_scr[...] += jnp.dot(jnp.concatenate(ps, axis=1), v_ref[...], preferred_element_type=F32)

    @pl.when(e == ne - 1)
    def _():
        o_ref[...] = h_ref[...] + acc_scr[...]


def _peer_dense(x, g, u, v, h, t, eb):
    n, d = x.shape
    ne = u.shape[0] // eb
    tok = lambda c: pl.BlockSpec((t, c), lambda i, e: (i, 0))
    return pl.pallas_call(
        _peer_dense_kernel,
        grid=(n // t, ne),
        in_specs=[tok(d),
                  pl.BlockSpec((None, t * PEER_STEP_KEYS, PEER_NKEYS), lambda i, e: (e, i, 0)),
                  pl.BlockSpec((eb, d), lambda i, e: (e, 0)),
                  pl.BlockSpec((eb, d), lambda i, e: (e, 0)),
                  pl.BlockSpec((t, d), lambda i, e: (i, 0), pipeline_mode=pl.Buffered(1))],
        out_specs=tok(d),
        out_shape=jax.ShapeDtypeStruct((n, d), F32),
        scratch_shapes=[pltpu.VMEM((t, d), F32)],
        compiler_params=_params(("parallel", "arbitrary")),
        name="peer_dense",
    )(x, g, u, v, h)


def _gla_kernel(x_ref, gin_ref, win_ref, wg2_ref, bg_ref, gon_ref, tri_ref, o_ref, state_scr):
    tc = x_ref.shape[0]
    c_len = GLA_CHUNK
    n_chunks = tc // c_len
    hk = GLA_HEADS * GLA_DK
    hv = GLA_HEADS * GLA_DV

    @pl.when(pl.program_id(1) == 0)
    def _():
        state_scr[...] = jnp.zeros(state_scr.shape, F32)

    hn = _rms(x_ref[...], gin_ref[...])
    proj = jnp.dot(hn.astype(BF16), win_ref[...], preferred_element_type=F32)
    q = proj[:, :hk] * (GLA_DK ** -0.5)
    k = proj[:, hk:2 * hk]
    g_low = proj[:, 2 * hk + 2 * hv:]
    z = jnp.dot(g_low, wg2_ref[...], precision=HIGHEST, preferred_element_type=F32) + bg_ref[...]
    log_a = (jnp.minimum(z, 0.0) - jnp.log1p(jnp.exp(-jnp.abs(z)))) * (1.0 / GLA_TAU)

    hi = log_a.astype(BF16)
    rest = log_a - hi.astype(F32)
    mid = rest.astype(BF16)
    lo = (rest - mid.astype(F32)).astype(BF16)
    cum = jnp.dot(tri_ref[...], jnp.concatenate([hi, mid, lo], axis=1), preferred_element_type=F32)
    b = cum[:, :hk] + cum[:, hk:2 * hk] + cum[:, 2 * hk:]
    b_last = jnp.concatenate(
        [jnp.broadcast_to(b[(c + 1) * c_len - 1:(c + 1) * c_len, :], (c_len, hk)) for c in range(n_chunks)],
        axis=0)
    q_dec = (q * jnp.exp(b)).astype(BF16)
    k_inv = k * jnp.exp(-b)
    dec = jnp.exp(b_last)
    k_rem = k_inv * dec
    k_inv = k_inv.astype(BF16)

    row_chunk = lax.broadcasted_iota(jnp.int32, (tc, GLA_DK), 0) // c_len
    col_chunk = lax.broadcasted_iota(jnp.int32, (GLA_DK, tc), 1) // c_len
    causal = tri_ref[...] > 0
    contract_last = (((1,), (1,)), ((), ()))
    for h in range(GLA_HEADS):
        ksl = slice(h * GLA_DK, (h + 1) * GLA_DK)
        vsl = slice(2 * hk + h * GLA_DV, 2 * hk + (h + 1) * GLA_DV)
        rsl = slice(2 * hk + hv + h * GLA_DV, 2 * hk + hv + (h + 1) * GLA_DV)
        qd = q_dec[:, ksl]
        vh = proj[:, vsl].astype(BF16)
        attn = lax.dot_general(qd, k_inv[:, ksl], contract_last, preferred_element_type=F32)
        o = jnp.dot(jnp.where(causal, attn, 0.0).astype(BF16), vh, preferred_element_type=F32)
        k_t = k_rem[:, ksl].T.astype(BF16)
        kv = jnp.dot(jnp.concatenate([jnp.where(col_chunk == c, k_t, 0.0) for c in range(n_chunks)], axis=0),
                     vh, preferred_element_type=F32)
        state = state_scr[h]
        states = []
        for c in range(n_chunks):
            states.append(state.astype(BF16))
            dec_col = jnp.broadcast_to(dec[c * c_len:c * c_len + 1, ksl], (8, GLA_DK)).T[:, 0:1]
            state = state * dec_col + kv[c * GLA_DK:(c + 1) * GLA_DK, :]
        state_scr[h] = state
        o = o + jnp.dot(jnp.concatenate([jnp.where(row_chunk == c, qd, 0.0) for c in range(n_chunks)], axis=1),
                        jnp.concatenate(states, axis=0), preferred_element_type=F32)
        rr = proj[:, rsl]
        o_ref[:, h * GLA_DV:(h + 1) * GLA_DV] = (_rms(o, gon_ref[...]) * (rr * jax.nn.sigmoid(rr))).astype(o_ref.dtype)


def _gla(x2, g_in, w, g_on, batch, seq, tc):
    n, d = x2.shape
    hk = GLA_HEADS * GLA_DK
    hv = GLA_HEADS * GLA_DV
    ns = seq // tc
    idx = np.arange(tc)
    tri = ((idx[:, None] // GLA_CHUNK == idx[None, :] // GLA_CHUNK) & (idx[None, :] <= idx[:, None]))
    return pl.pallas_call(
        _gla_kernel,
        grid=(batch, ns),
        in_specs=[pl.BlockSpec((tc, d), lambda b, s: (b * ns + s, 0)), _full((1, d)), _full(w["win"].shape),
                  _full(w["wg2"].shape), _full((1, hk)), _full((1, GLA_DV)), _full((tc, tc))],
        out_specs=pl.BlockSpec((tc, hv), lambda b, s: (b * ns + s, 0)),
        out_shape=jax.ShapeDtypeStruct((n, hv), BF16),
        scratch_shapes=[pltpu.VMEM((GLA_HEADS, GLA_DK, GLA_DV), F32)],
        compiler_params=_params(("parallel", "arbitrary")),
        name="gla",
    )(x2, g_in, w["win"], w["wg2"], w["bg"], g_on, jnp.asarray(tri, BF16))


def _prep_mla(w_down, g_q_lat, w_uq, g_kv_lat, w_ukv, g_qn, g_kn):
    d = w_down.shape[0]
    lat = MLA_Q_RANK + MLA_KV_RANK
    z = lambda c: jnp.zeros((d, c), F32)
    wd = jnp.concatenate([w_down[:, :lat], z(MLA_NOPE), w_down[:, lat:], z(LANES - MLA_QK)], axis=1)
    pad_head = LANES - MLA_QK
    wuq = jnp.pad(w_uq.reshape(MLA_Q_RANK, MLA_HEADS, MLA_QK), ((0, 0), (0, 0), (0, pad_head)))
    wukv = w_ukv.reshape(MLA_KV_RANK, MLA_HEADS, MLA_NOPE + MLA_V)
    wuk = jnp.pad(wukv[:, :, :MLA_NOPE], ((0, 0), (0, 0), (0, LANES - MLA_NOPE)))
    wuv = jnp.pad(wukv[:, :, MLA_NOPE:], ((0, 0), (0, 0), (0, LANES - MLA_V)))
    half = MLA_ROPE // 2
    inv_freq = ROPE_THETA ** (-jnp.arange(half, dtype=F32) / half)
    zero = lambda c: jnp.zeros((c,), F32)
    vone = jnp.tile(jnp.concatenate([zero(MLA_V), jnp.ones((LANES - MLA_V,), F32)]), MLA_HEADS)
    freq = jnp.concatenate([zero(MLA_NOPE), inv_freq, inv_freq, zero(pad_head)])
    rotw = np.zeros((2 * LANES, 2 * LANES), np.float32)
    for lane in range(MLA_NOPE, MLA_NOPE + half):
        rotw[lane + half, lane] = -1.0
        rotw[lane, lane + half] = 1.0
    rotw[LANES:, LANES:] = 1.0
    scale = MLA_QK ** -0.5 * math.log2(math.e)
    return {
        "wd": wd.astype(BF16),
        "gql": g_q_lat.reshape(1, -1), "gkl": g_kv_lat.reshape(1, -1),
        "wuq": wuq.reshape(MLA_Q_RANK, MLA_HEADS * LANES).astype(BF16),
        "wuk": wuk.reshape(MLA_KV_RANK, MLA_HEADS * LANES).astype(BF16),
        "wuv": wuv.reshape(MLA_KV_RANK, MLA_HEADS * LANES).astype(BF16),
        "vone": vone.reshape(1, MLA_HEADS * LANES),
        "gqn": (jnp.pad(g_qn, (0, pad_head)) * scale).reshape(1, LANES),
        "gkn": jnp.pad(g_kn, (0, pad_head)).reshape(1, LANES),
        "freq": freq.reshape(1, LANES), "rotw": jnp.asarray(rotw, BF16),
    }


def _prep_gla(w_in, w_g2, b_g):
    pad = LANES - GLA_GATE_RANK
    return {
        "win": jnp.pad(w_in, ((0, 0), (0, pad))).astype(BF16),
        "wg2": jnp.pad(w_g2, ((0, pad), (0, 0))),
        "bg": b_g.reshape(1, -1),
    }


def _prep_peer(w_query, sub_keys, u_tab, v_tab):
    zk = jnp.zeros((PEER_NKEYS, PEER_HALF), F32)
    return {
        "wq": _split3_const(w_query, 0),
        "k0": _split3_const(jnp.concatenate([sub_keys[0], zk], axis=1), 1),
        "k1": _split3_const(jnp.concatenate([zk, sub_keys[1]], axis=1), 1),
        "u": u_tab.astype(BF16),
        "v": v_tab.astype(BF16),
    }


def _peer_layer(res, a, w_o, g, pw, tiles):
    h, xn, idx, a2, th, s1, b2 = _peer_route(res, a, w_o, g.reshape(1, -1), pw["wq"], pw["k0"], pw["k1"],
                                             tiles["route"])
    gate = _peer_weights(idx, a2, th, s1, b2, tiles["weights"])
    gate = gate.reshape(gate.shape[0], -1, PEER_NKEYS)
    return _peer_dense(xn, gate, pw["u"], pw["v"], h, tiles["dense_t"], PEER_STEP_KEYS * PEER_NKEYS)


def _tiles(seq):
    pick = lambda want: math.gcd(want, seq)
    return {"proj": pick(512), "attn": pick(2048), "route": pick(512), "weights": pick(128),
            "dense_t": pick(1024), "gla": pick(512)}


def kernel(x, positions, attn_norm_g, ffn_norm_g, mla_w_down, mla_g_q_lat, mla_w_uq, mla_g_kv_lat,
           mla_w_ukv, mla_g_qn, mla_g_kn, mla_w_o, gla_w_in, gla_w_g2, gla_b_g, gla_g_on, gla_w_o,
           peer_w_query, peer_sub_keys, peer_u, peer_v):
    batch, seq, d = x.shape
    n = batch * seq
    depth = attn_norm_g.shape[0]
    tiles = _tiles(seq)
    h = x.reshape(n, d)
    pos = positions.reshape(n, 1).astype(F32)
    for i in range(depth):
        j = i // 2
        g_in = attn_norm_g[i].reshape(1, d)
        if i % 2 == 0:
            w = _prep_mla(mla_w_down[j], mla_g_q_lat[j], mla_w_uq[j], mla_g_kv_lat[j], mla_w_ukv[j],
                          mla_g_qn[j], mla_g_kn[j])
            q, k, v = _mla_proj(h, pos, g_in, w, tiles["proj"])
            a = _attention(q, k, v, batch, seq, tiles["attn"])
            w_o = mla_w_o[j].astype(BF16)
        else:
            w = _prep_gla(gla_w_in[j], gla_w_g2[j], gla_b_g[j])
            a = _gla(h, g_in, w, gla_g_on[j].reshape(1, -1), batch, seq, tiles["gla"])
            w_o = gla_w_o[j].astype(BF16)
        pw = _prep_peer(peer_w_query[i], peer_sub_keys[i], peer_u[i], peer_v[i])
        h = _peer_layer(h, a, w_o, ffn_norm_g[i], pw, tiles)
    return h.reshape(batch, seq, d)
```

```python
import math

import jax
import jax.numpy as jnp
import numpy as np
from jax import lax
from jax.experimental import pallas as pl
from jax.experimental.pallas import tpu as pltpu

F32 = jnp.float32
BF16 = jnp.bfloat16
HIGHEST = lax.Precision.HIGHEST

LANES = 128
NORM_EPS = 1e-6
VMEM_LIMIT = 56 * 1024 * 1024

MLA_HEADS = 16
MLA_Q_RANK = 384
MLA_KV_RANK = 256
MLA_NOPE = 64
MLA_ROPE = 32
MLA_QK = MLA_NOPE + MLA_ROPE
MLA_V = 64
ROPE_THETA = 10000.0
GLA_HEADS = 4
GLA_DK = 128
GLA_DV = 256
GLA_GATE_RANK = 16
GLA_TAU = 16.0
GLA_CHUNK = 64
PEER_HEADS = 8
PEER_NKEYS = 128
PEER_HALF = 64
PEER_TOPK = 16
NEG_INF = float("-inf")
PEER_STEP_KEYS = 8
ATTN_ROW_CHUNK = 1024


def _params(sem):
    return pltpu.CompilerParams(dimension_semantics=sem, vmem_limit_bytes=VMEM_LIMIT)


def _rms(x, g, n=None):
    n = x.shape[-1] if n is None else n
    ss = jnp.sum(x * x, axis=-1, keepdims=True)
    return x * lax.rsqrt(ss * (1.0 / n) + NORM_EPS) * g


def _split3(x):
    hi = x.astype(BF16)
    lo = (x - hi.astype(F32)).astype(BF16)
    return jnp.concatenate([hi, lo, hi], axis=-1)


def _split3_const(w, axis):
    bits = lax.bitcast_convert_type(w, jnp.uint32) & jnp.uint32(0xFFFF0000)
    hi = lax.bitcast_convert_type(bits, F32)
    lo = (w - hi).astype(BF16)
    hi = hi.astype(BF16)
    return jnp.concatenate([hi, hi, lo], axis=axis)


def _full(shape):
    nd = len(shape)
    return pl.BlockSpec(shape, lambda *_: (0,) * nd)


def _mla_proj_kernel(x_ref, pos_ref, gin_ref, wd_ref, gql_ref, wuq_ref, gkl_ref, wuk_ref, wuv_ref,
                     gqn_ref, gkn_ref, freq_ref, rotw_ref, vone_ref, q_ref, k_ref, v_ref):
    x = x_ref[...]
    hn = _rms(x, gin_ref[...])
    down = jnp.dot(hn.astype(BF16), wd_ref[...], preferred_element_type=F32)
    c_q = _rms(down[:, :MLA_Q_RANK], gql_ref[...]).astype(BF16)
    c_kv = _rms(down[:, MLA_Q_RANK:MLA_Q_RANK + MLA_KV_RANK], gkl_ref[...]).astype(BF16)
    kr = down[:, MLA_Q_RANK + MLA_KV_RANK:]
    qf = jnp.dot(c_q, wuq_ref[...], preferred_element_type=F32)
    kf = jnp.dot(c_kv, wuk_ref[...], preferred_element_type=F32)
    v_ref[...] = (jnp.dot(c_kv, wuv_ref[...], preferred_element_type=F32) + vone_ref[...]).astype(v_ref.dtype)

    ang = pos_ref[...] * freq_ref[...]
    cos = jnp.cos(ang)
    sin = jnp.sin(ang)

    def head(t, g):
        tg = t * g
        res = jnp.dot(jnp.concatenate([tg.astype(BF16), (t * t).astype(BF16)], axis=1), rotw_ref[...],
                      preferred_element_type=F32)
        r = lax.rsqrt(res[:, LANES:] * (1.0 / MLA_QK) + NORM_EPS)
        return r * (tg * cos + res[:, :LANES] * sin)

    for h in range(MLA_HEADS):
        sl = slice(h * LANES, (h + 1) * LANES)
        q_ref[:, sl] = head(qf[:, sl], gqn_ref[...]).astype(q_ref.dtype)
        k_ref[:, sl] = head(kf[:, sl] + kr, gkn_ref[...]).astype(k_ref.dtype)


def _mla_proj(x2, pos2, g_in, w, tm):
    n, d = x2.shape
    hq = MLA_HEADS * LANES
    row = lambda c: pl.BlockSpec((tm, c), lambda i: (i, 0))
    return pl.pallas_call(
        _mla_proj_kernel,
        grid=(n // tm,),
        in_specs=[row(d), row(1), _full((1, d)), _full(w["wd"].shape), _full((1, MLA_Q_RANK)),
                  _full(w["wuq"].shape), _full((1, MLA_KV_RANK)), _full(w["wuk"].shape),
                  _full(w["wuv"].shape), _full((1, LANES)), _full((1, LANES)), _full((1, LANES)),
                  _full(w["rotw"].shape), _full((1, hq))],
        out_specs=[row(hq), row(hq), row(hq)],
        out_shape=[jax.ShapeDtypeStruct((n, hq), BF16)] * 3,
        compiler_params=_params(("parallel",)),
        name="mla_proj",
    )(x2, pos2, g_in, w["wd"], w["gql"], w["wuq"], w["gkl"], w["wuk"], w["wuv"], w["gqn"], w["gkn"],
      w["freq"], w["rotw"], w["vone"])


def _attn_kernel(qi_ref, ki_ref, q_ref, k_ref, v_ref, o_ref, m_scr, acc_scr):
    p_id = pl.program_id(2)
    qi = qi_ref[p_id]
    ki = ki_ref[p_id]
    tq = q_ref.shape[0]
    tk = k_ref.shape[0]
    rq = min(ATTN_ROW_CHUNK, tq)

    @pl.when(ki == 0)
    def _():
        m_scr[...] = jnp.full(m_scr.shape, NEG_INF, F32)
        acc_scr[...] = jnp.zeros(acc_scr.shape, F32)

    def step(diagonal):
        for hh in range(2):
            sl = slice(hh * LANES, (hh + 1) * LANES)
            for r in range(tq // rq):
                rows = slice(r * rq, (r + 1) * rq)
                nk = (r + 1) * rq if diagonal else tk
                s = lax.dot_general(q_ref[rows, sl], k_ref[:nk, sl], (((1,), (1,)), ((), ())),
                                    preferred_element_type=F32)
                if diagonal:
                    row = lax.broadcasted_iota(jnp.int32, (rq, rq), 0)
                    col = lax.broadcasted_iota(jnp.int32, (rq, rq), 1)
                    tail = jnp.where(col <= row, s[:, r * rq:], NEG_INF)
                    s = tail if r == 0 else jnp.concatenate([s[:, :r * rq], tail], axis=1)
                m_prev = m_scr[hh, rows]
                m_new = jnp.maximum(m_prev, jnp.max(s, axis=-1, keepdims=True))
                alpha = jnp.exp2(m_prev - m_new)
                p = jnp.concatenate(
                    [jnp.exp2(s[:, c * LANES:(c + 1) * LANES] - m_new).astype(BF16)
                     for c in range(nk // LANES)], axis=1)
                pv = jnp.dot(p, v_ref[:nk, sl], preferred_element_type=F32)
                acc_scr[hh, rows] = alpha * acc_scr[hh, rows] + pv
                m_scr[hh, rows] = m_new

    @pl.when(ki < qi)
    def _():
        step(False)

    @pl.when(ki == qi)
    def _():
        step(True)
        lane = lax.broadcasted_iota(jnp.int32, (tq, LANES), 1)
        a0 = acc_scr[0]
        a1 = acc_scr[1]
        o0 = a0 / pltpu.roll(a0, MLA_V, 1)
        o1 = pltpu.roll(a1, MLA_V, 1) / a1
        o_ref[...] = jnp.where(lane < MLA_V, o0, o1).astype(o_ref.dtype)


def _attention(q, k, v, batch, seq, tq):
    nq = seq // tq
    qi_tab = np.array([qi for qi in range(nq) for _ in range(qi + 1)], np.int32)
    ki_tab = np.array([ki for qi in range(nq) for ki in range(qi + 1)], np.int32)
    npairs = len(qi_tab)
    hp = MLA_HEADS // 2
    blk = lambda tab: pl.BlockSpec((tq, 2 * LANES), lambda b, h, p, qt, kt: (b * nq + (qt, kt)[tab][p], h))
    grid_spec = pltpu.PrefetchScalarGridSpec(
        num_scalar_prefetch=2,
        grid=(batch, hp, npairs),
        in_specs=[blk(0), blk(1), blk(1)],
        out_specs=pl.BlockSpec((tq, 2 * MLA_V), lambda b, h, p, qt, kt: (b * nq + qt[p], h)),
        scratch_shapes=[pltpu.VMEM((2, tq, LANES), F32), pltpu.VMEM((2, tq, LANES), F32)],
    )
    return pl.pallas_call(
        _attn_kernel,
        grid_spec=grid_spec,
        out_shape=jax.ShapeDtypeStruct((batch * seq, MLA_HEADS * MLA_V), BF16),
        compiler_params=_params(("parallel", "parallel", "arbitrary")),
        name="mla_attention",
    )(jnp.asarray(qi_tab), jnp.asarray(ki_tab), q, k, v)


def _sort_network(n):
    pairs = []
    p = 1
    while p < n:
        k = p
        while k >= 1:
            for j in range(k % p, n - k, 2 * k):
                for i in range(min(k, n - j - k)):
                    if (i + j) // (2 * p) == (i + j + k) // (2 * p):
                        pairs.append((i + j, i + j + k))
            k //= 2
        p *= 2
    return pairs


def _sorted_top(cur, n, val_scr, idx_scr=None):
    sub = 8
    groups = cur.shape[0] // sub
    track = idx_scr is not None
    base = lax.broadcasted_iota(jnp.int32, (sub, LANES), 0).astype(F32)
    network = _sort_network(groups)
    for c in range(cur.shape[1] // LANES):
        csl = slice(c * LANES, (c + 1) * LANES)
        vals = [cur[g * sub:(g + 1) * sub, csl] for g in range(groups)]
        if track:
            idxs = [base + float(g * sub) for g in range(groups)]
        for a, b in network:
            if track:
                keep = vals[a] >= vals[b]
                idxs[a], idxs[b] = jnp.where(keep, idxs[a], idxs[b]), jnp.where(keep, idxs[b], idxs[a])
            vals[a], vals[b] = jnp.maximum(vals[a], vals[b]), jnp.minimum(vals[a], vals[b])
        for r in range(n):
            m = jnp.max(vals[0], axis=0, keepdims=True)
            val_scr[r:r + 1, csl] = m
            hit = vals[0] == m
            if track:
                first = jnp.min(jnp.where(hit, base, float(sub)), axis=0, keepdims=True)
                hit = base == first
                idx_scr[r:r + 1, csl] = jnp.max(jnp.where(hit, idxs[0], -1.0), axis=0, keepdims=True)
            depth = min(n - 1 - r, groups)
            for k in range(depth):
                below = vals[k + 1] if k + 1 < groups else NEG_INF
                vals[k] = jnp.where(hit, below, vals[k])
                if track and k + 1 < groups:
                    idxs[k] = jnp.where(hit, idxs[k + 1], idxs[k])


def _peer_route_kernel(res_ref, a_ref, wo_ref, g_ref, wq_ref, k0_ref, k1_ref,
                       h_ref, x_ref, idx_ref, a2_ref, th_ref, s1_ref, b2_ref,
                       a_scr, b_scr, c_scr, i_scr, idx_all, a2_all):
    tr = res_ref.shape[0]
    h = res_ref[...] + jnp.dot(a_ref[...], wo_ref[...], preferred_element_type=F32)
    h_ref[...] = h
    hn = _rms(h, g_ref[...])
    x_ref[...] = (hn * (1.0 / math.sqrt(2.0))).astype(x_ref.dtype)
    q = jnp.dot(_split3(hn), wq_ref[...], preferred_element_type=F32)
    q3 = [_split3(q[:, hd * LANES:(hd + 1) * LANES]) for hd in range(PEER_HEADS)]
    nk = PEER_TOPK + 1
    rows = lax.broadcasted_iota(jnp.int32, (8, tr), 0)
    a_scr[...] = jnp.full(a_scr.shape, NEG_INF, F32)
    b_scr[...] = jnp.full(b_scr.shape, NEG_INF, F32)
    contract_last = (((1,), (1,)), ((), ()))
    log2e = math.log2(math.e)
    for hd in range(PEER_HEADS):
        s0 = lax.dot_general(k0_ref[...], q3[hd], contract_last, preferred_element_type=F32)
        s1 = lax.dot_general(k1_ref[...], q3[hd], contract_last, preferred_element_type=F32)
        _sorted_top(s0, nk, a_scr, i_scr)
        _sorted_top(s1, nk, b_scr)
        pieces = [b_scr[...] + a_scr[0:1, :]]
        r = 1
        while nk // (r + 1) >= 2:
            pieces.append(jnp.where(rows < nk // (r + 1), b_scr[0:8, :] + a_scr[r:r + 1, :], NEG_INF))
            r += 1
        assert r == 8
        pieces.append(a_scr[8:24, :] + b_scr[0:1, :])
        cand = jnp.concatenate(pieces, axis=0)
        padded = jnp.concatenate([cand, jnp.full((PEER_NKEYS - cand.shape[0], tr), NEG_INF, F32)], axis=0)
        _sorted_top(padded, nk, c_scr)
        best = c_scr[0:1, :]
        tau = 0.5 * (c_scr[PEER_TOPK - 1:PEER_TOPK, :] + c_scr[PEER_TOPK:PEER_TOPK + 1, :])
        z = jnp.sum(jnp.where(cand >= tau, jnp.exp(cand - best), 0.0), axis=0, keepdims=True)
        a_max = a_scr[0:1, :]
        a_rel = (a_scr[0:PEER_TOPK, :] - a_max) * log2e
        s1_rel = (s1 - (best - a_max + jnp.log(z) + 0.5 * math.log(2.0))) * log2e
        thr = (tau - (best + jnp.log(z) + 0.5 * math.log(2.0))) * log2e
        slots = slice(hd * PEER_TOPK, (hd + 1) * PEER_TOPK)
        idx_all[slots, :] = i_scr[0:PEER_TOPK, :]
        a2_all[slots, :] = jnp.exp2(a_rel)
        th_ref[slots, :] = thr - a_rel
        s1t = s1_rel.T
        s1_ref[hd] = s1t
        b2_ref[hd] = jnp.exp2(s1t)
    idx_ref[...] = idx_all[...].T
    a2_ref[...] = a2_all[...].T


def _peer_route(res, a, w_o, g, wq, k0, k1, tr):
    n, d = res.shape
    da = a.shape[1]
    nh = PEER_HEADS
    ns = nh * PEER_TOPK
    row = lambda c: pl.BlockSpec((tr, c), lambda i: (i, 0))
    per_head = pl.BlockSpec((nh, tr, PEER_NKEYS), lambda i: (0, i, 0))
    return pl.pallas_call(
        _peer_route_kernel,
        grid=(n // tr,),
        in_specs=[row(d), row(da), _full(w_o.shape), _full((1, d)), _full(wq.shape), _full(k0.shape),
                  _full(k1.shape)],
        out_specs=[row(d), row(d), row(ns), row(ns), pl.BlockSpec((ns, tr), lambda i: (0, i)),
                   per_head, per_head],
        out_shape=[jax.ShapeDtypeStruct((n, d), F32), jax.ShapeDtypeStruct((n, d), BF16),
                   jax.ShapeDtypeStruct((n, ns), F32), jax.ShapeDtypeStruct((n, ns), F32),
                   jax.ShapeDtypeStruct((ns, n), F32),
                   jax.ShapeDtypeStruct((nh, n, PEER_NKEYS), F32),
                   jax.ShapeDtypeStruct((nh, n, PEER_NKEYS), F32)],
        scratch_shapes=[pltpu.VMEM((24, tr), F32)] * 4 + [pltpu.VMEM((ns, tr), F32)] * 2,
        compiler_params=_params(("parallel",)),
        name="peer_route",
    )(res, a, w_o, g, wq, k0, k1)


def _peer_weights_kernel(idx_ref, a2_ref, th_ref, s1_ref, b2_ref, g_ref):
    tg = idx_ref.shape[0]
    ns = th_ref.shape[0]
    sub = lax.broadcasted_iota(jnp.int32, (PEER_NKEYS, ns), 0).astype(F32)

    def head_rows(ref, t):
        return jnp.concatenate([jnp.broadcast_to(ref[hd, t:t + 1, :], (PEER_TOPK, PEER_NKEYS))
                                for hd in range(PEER_HEADS)], axis=0)

    for t in range(tg):
        first = jnp.where(sub == idx_ref[t:t + 1, :], a2_ref[t:t + 1, :], 0.0).astype(BF16)
        thr = jnp.broadcast_to(th_ref[:, t:t + 1], (ns, PEER_NKEYS))
        second = jnp.where(head_rows(s1_ref, t) >= thr, head_rows(b2_ref, t), 0.0).astype(BF16)
        gt = jnp.dot(first, second, preferred_element_type=F32)
        for blk in range(PEER_NKEYS // PEER_STEP_KEYS):
            g_ref[blk, t] = gt[blk * PEER_STEP_KEYS:(blk + 1) * PEER_STEP_KEYS, :]


def _peer_weights(idx, a2, th, s1, b2, tg):
    n, ns = idx.shape
    nh = PEER_HEADS
    nb = PEER_NKEYS // PEER_STEP_KEYS
    row = pl.BlockSpec((tg, ns), lambda i: (i, 0))
    per_head = pl.BlockSpec((nh, tg, PEER_NKEYS), lambda i: (0, i, 0))
    return pl.pallas_call(
        _peer_weights_kernel,
        grid=(n // tg,),
        in_specs=[row, row, pl.BlockSpec((ns, tg), lambda i: (0, i)), per_head, per_head],
        out_specs=pl.BlockSpec((nb, tg, PEER_STEP_KEYS, PEER_NKEYS), lambda i: (0, i, 0, 0)),
        out_shape=jax.ShapeDtypeStruct((nb, n, PEER_STEP_KEYS, PEER_NKEYS), F32),
        compiler_params=_params(("parallel",)),
        name="peer_weights",
    )(idx, a2, th, s1, b2)


def _peer_dense_kernel(x_ref, g_ref, u_ref, v_ref, h_ref, o_ref, acc_scr):
    e = pl.program_id(1)
    ne = pl.num_programs(1)
    t = x_ref.shape[0]
    rows_per_step = PEER_STEP_KEYS

    @pl.when(e == 0)
    def _():
        acc_scr[...] = jnp.zeros(acc_scr.shape, F32)

    y = lax.dot_general(x_ref[...], u_ref[...], (((1,), (1,)), ((), ())),
                        preferred_element_type=F32)
    ps = []
    for ii in range(rows_per_step):
        yc = y[:, ii * PEER_NKEYS:(ii + 1) * PEER_NKEYS]
        gi = g_ref[pl.ds(ii, t, stride=rows_per_step), :]
        ps.append((gi * (yc * (1.0 + lax.erf(yc)))).astype(BF16))
    acc_scr[...] += jnp.dot(jnp.concatenate(ps, axis=1), v_ref[...], preferred_element_type=F32)

    @pl.when(e == ne - 1)
    def _():
        o_ref[...] = h_ref[...] + acc_scr[...]


def _peer_dense(x, g, u, v, h, t, eb):
    n, d = x.shape
    ne = u.shape[0] // eb
    tok = lambda c: pl.BlockSpec((t, c), lambda i, e: (i, 0))
    return pl.pallas_call(
        _peer_dense_kernel,
        grid=(n // t, ne),
        in_specs=[tok(d),
                  pl.BlockSpec((None, t * PEER_STEP_KEYS, PEER_NKEYS), lambda i, e: (e, i, 0)),
                  pl.BlockSpec((eb, d), lambda i, e: (e, 0)),
                  pl.BlockSpec((eb, d), lambda i, e: (e, 0)),
                  pl.BlockSpec((t, d), lambda i, e: (i, 0), pipeline_mode=pl.Buffered(1))],
        out_specs=tok(d),
        out_shape=jax.ShapeDtypeStruct((n, d), F32),
        scratch_shapes=[pltpu.VMEM((t, d), F32)],
        compiler_params=_params(("parallel", "arbitrary")),
        name="peer_dense",
    )(x, g, u, v, h)


def _gla_kernel(x_ref, gin_ref, win_ref, wg2_ref, bg_ref, gon_ref, tri_ref, o_ref, state_scr):
    tc = x_ref.shape[0]
    c_len = GLA_CHUNK
    n_chunks = tc // c_len
    hk = GLA_HEADS * GLA_DK
    hv = GLA_HEADS * GLA_DV

    @pl.when(pl.program_id(1) == 0)
    def _():
        state_scr[...] = jnp.zeros(state_scr.shape, F32)

    hn = _rms(x_ref[...], gin_ref[...])
    proj = jnp.dot(hn.astype(BF16), win_ref[...], preferred_element_type=F32)
    q = proj[:, :hk] * (GLA_DK ** -0.5)
    k = proj[:, hk:2 * hk]
    g_low = proj[:, 2 * hk + 2 * hv:]
    z = jnp.dot(g_low, wg2_ref[...], precision=HIGHEST, preferred_element_type=F32) + bg_ref[...]
    log_a = (jnp.minimum(z, 0.0) - jnp.log1p(jnp.exp(-jnp.abs(z)))) * (1.0 / GLA_TAU)

    hi = log_a.astype(BF16)
    rest = log_a - hi.astype(F32)
    mid = rest.astype(BF16)
    lo = (rest - mid.astype(F32)).astype(BF16)
    cum = jnp.dot(tri_ref[...], jnp.concatenate([hi, mid, lo], axis=1), preferred_element_type=F32)
    b = cum[:, :hk] + cum[:, hk:2 * hk] + cum[:, 2 * hk:]
    b_last = jnp.concatenate(
        [jnp.broadcast_to(b[(c + 1) * c_len - 1:(c + 1) * c_len, :], (c_len, hk)) for c in range(n_chunks)],
        axis=0)
    q_dec = (q * jnp.exp(b)).astype(BF16)
    k_inv = k * jnp.exp(-b)
    dec = jnp.exp(b_last)
    k_rem = k_inv * dec
    k_inv = k_inv.astype(BF16)

    row_chunk = lax.broadcasted_iota(jnp.int32, (tc, GLA_DK), 0) // c_len
    col_chunk = lax.broadcasted_iota(jnp.int32, (GLA_DK, tc), 1) // c_len
    causal = tri_ref[...] > 0
    contract_last = (((1,), (1,)), ((), ()))
    for h in range(GLA_HEADS):
        ksl = slice(h * GLA_DK, (h + 1) * GLA_DK)
        vsl = slice(2 * hk + h * GLA_DV, 2 * hk + (h + 1) * GLA_DV)
        rsl = slice(2 * hk + hv + h * GLA_DV, 2 * hk + hv + (h + 1) * GLA_DV)
        qd = q_dec[:, ksl]
        vh = proj[:, vsl].astype(BF16)
        attn = lax.dot_general(qd, k_inv[:, ksl], contract_last, preferred_element_type=F32)
        o = jnp.dot(jnp.where(causal, attn, 0.0).astype(BF16), vh, preferred_element_type=F32)
        k_t = k_rem[:, ksl].T.astype(BF16)
        kv = jnp.dot(jnp.concatenate([jnp.where(col_chunk == c, k_t, 0.0) for c in range(n_chunks)], axis=0),
                     vh, preferred_element_type=F32)
        state = state_scr[h]
        states = []
        for c in range(n_chunks):
            states.append(state.astype(BF16))
            dec_col = jnp.broadcast_to(dec[c * c_len:c * c_len + 1, ksl], (8, GLA_DK)).T[:, 0:1]
            state = state * dec_col + kv[c * GLA_DK:(c + 1) * GLA_DK, :]
        state_scr[h] = state
        o = o + jnp.dot(jnp.concatenate([jnp.where(row_chunk == c, qd, 0.0) for c in range(n_chunks)], axis=1),
                        jnp.concatenate(states, axis=0), preferred_element_type=F32)
        rr = proj[:, rsl]
        o_ref[:, h * GLA_DV:(h + 1) * GLA_DV] = (_rms(o, gon_ref[...]) * (rr * jax.nn.sigmoid(rr))).astype(o_ref.dtype)


def _gla(x2, g_in, w, g_on, batch, seq, tc):
    n, d = x2.shape
    hk = GLA_HEADS * GLA_DK
    hv = GLA_HEADS * GLA_DV
    ns = seq // tc
    idx = np.arange(tc)
    tri = ((idx[:, None] // GLA_CHUNK == idx[None, :] // GLA_CHUNK) & (idx[None, :] <= idx[:, None]))
    return pl.pallas_call(
        _gla_kernel,
        grid=(batch, ns),
        in_specs=[pl.BlockSpec((tc, d), lambda b, s: (b * ns + s, 0)), _full((1, d)), _full(w["win"].shape),
                  _full(w["wg2"].shape), _full((1, hk)), _full((1, GLA_DV)), _full((tc, tc))],
        out_specs=pl.BlockSpec((tc, hv), lambda b, s: (b * ns + s, 0)),
        out_shape=jax.ShapeDtypeStruct((n, hv), BF16),
        scratch_shapes=[pltpu.VMEM((GLA_HEADS, GLA_DK, GLA_DV), F32)],
        compiler_params=_params(("parallel", "arbitrary")),
        name="gla",
    )(x2, g_in, w["win"], w["wg2"], w["bg"], g_on, jnp.asarray(tri, BF16))


def _prep_mla(w_down, g_q_lat, w_uq, g_kv_lat, w_ukv, g_qn, g_kn):
    d = w_down.shape[0]
    lat = MLA_Q_RANK + MLA_KV_RANK
    z = lambda c: jnp.zeros((d, c), F32)
    wd = jnp.concatenate([w_down[:, :lat], z(MLA_NOPE), w_down[:, lat:], z(LANES - MLA_QK)], axis=1)
    pad_head = LANES - MLA_QK
    wuq = jnp.pad(w_uq.reshape(MLA_Q_RANK, MLA_HEADS, MLA_QK), ((0, 0), (0, 0), (0, pad_head)))
    wukv = w_ukv.reshape(MLA_KV_RANK, MLA_HEADS, MLA_NOPE + MLA_V)
    wuk = jnp.pad(wukv[:, :, :MLA_NOPE], ((0, 0), (0, 0), (0, LANES - MLA_NOPE)))
    wuv = jnp.pad(wukv[:, :, MLA_NOPE:], ((0, 0), (0, 0), (0, LANES - MLA_V)))
    half = MLA_ROPE // 2
    inv_freq = ROPE_THETA ** (-jnp.arange(half, dtype=F32) / half)
    zero = lambda c: jnp.zeros((c,), F32)
    vone = jnp.tile(jnp.concatenate([zero(MLA_V), jnp.ones((LANES - MLA_V,), F32)]), MLA_HEADS)
    freq = jnp.concatenate([zero(MLA_NOPE), inv_freq, inv_freq, zero(pad_head)])
    rotw = np.zeros((2 * LANES, 2 * LANES), np.float32)
    for lane in range(MLA_NOPE, MLA_NOPE + half):
        rotw[lane + half, lane] = -1.0
        rotw[lane, lane + half] = 1.0
    rotw[LANES:, LANES:] = 1.0
    scale = MLA_QK ** -0.5 * math.log2(math.e)
    return {
        "wd": wd.astype(BF16),
        "gql": g_q_lat.reshape(1, -1), "gkl": g_kv_lat.reshape(1, -1),
        "wuq": wuq.reshape(MLA_Q_RANK, MLA_HEADS * LANES).astype(BF16),
        "wuk": wuk.reshape(MLA_KV_RANK, MLA_HEADS * LANES).astype(BF16),
        "wuv": wuv.reshape(MLA_KV_RANK, MLA_HEADS * LANES).astype(BF16),
        "vone": vone.reshape(1, MLA_HEADS * LANES),
        "gqn": (jnp.pad(g_qn, (0, pad_head)) * scale).reshape(1, LANES),
        "gkn": jnp.pad(g_kn, (0, pad_head)).reshape(1, LANES),
        "freq": freq.reshape(1, LANES), "rotw": jnp.asarray(rotw, BF16),
    }


def _prep_gla(w_in, w_g2, b_g):
    pad = LANES - GLA_GATE_RANK
    return {
        "win": jnp.pad(w_in, ((0, 0), (0, pad))).astype(BF16),
        "wg2": jnp.pad(w_g2, ((0, pad), (0, 0))),
        "bg": b_g.reshape(1, -1),
    }


def _prep_peer(w_query, sub_keys, u_tab, v_tab):
    zk = jnp.zeros((PEER_NKEYS, PEER_HALF), F32)
    return {
        "wq": _split3_const(w_query, 0),
        "k0": _split3_const(jnp.concatenate([sub_keys[0], zk], axis=1), 1),
        "k1": _split3_const(jnp.concatenate([zk, sub_keys[1]], axis=1), 1),
        "u": u_tab.astype(BF16),
        "v": v_tab.astype(BF16),
    }


def _peer_layer(res, a, w_o, g, pw, tiles):
    h, xn, idx, a2, th, s1, b2 = _peer_route(res, a, w_o, g.reshape(1, -1), pw["wq"], pw["k0"], pw["k1"],
                                             tiles["route"])
    gate = _peer_weights(idx, a2, th, s1, b2, tiles["weights"])
    gate = gate.reshape(gate.shape[0], -1, PEER_NKEYS)
    return _peer_dense(xn, gate, pw["u"], pw["v"], h, tiles["dense_t"], PEER_STEP_KEYS * PEER_NKEYS)


def _tiles(seq):
    pick = lambda want: math.gcd(want, seq)
    return {"proj": pick(512), "attn": pick(2048), "route": pick(512), "weights": pick(128),
            "dense_t": pick(1024), "gla": pick(256)}


def kernel(x, positions, attn_norm_g, ffn_norm_g, mla_w_down, mla_g_q_lat, mla_w_uq, mla_g_kv_lat,
           mla_w_ukv, mla_g_qn, mla_g_kn, mla_w_o, gla_w_in, gla_w_g2, gla_b_g, gla_g_on, gla_w_o,
           peer_w_query, peer_sub_keys, peer_u, peer_v):
    batch, seq, d = x.shape
    n = batch * seq
    depth = attn_norm_g.shape[0]
    tiles = _tiles(seq)
    h = x.reshape(n, d)
    pos = positions.reshape(n, 1).astype(F32)
    for i in range(depth):
        j = i // 2
        g_in = attn_norm_g[i].reshape(1, d)
        if i % 2 == 0:
            w = _prep_mla(mla_w_down[j], mla_g_q_lat[j], mla_w_uq[j], mla_g_kv_lat[j], mla_w_ukv[j],
                          mla_g_qn[j], mla_g_kn[j])
            q, k, v = _mla_proj(h, pos, g_in, w, tiles["proj"])
            a = _attention(q, k, v, batch, seq, tiles["attn"])
            w_o = mla_w_o[j].astype(BF16)
        else:
            w = _prep_gla(gla_w_in[j], gla_w_g2[j], gla_b_g[j])
            a = _gla(h, g_in, w, gla_g_on[j].reshape(1, -1), batch, seq, tiles["gla"])
            w_o = gla_w_o[j].astype(BF16)
        pw = _prep_peer(peer_w_query[i], peer_sub_keys[i], peer_u[i], peer_v[i])
        h = _peer_layer(h, a, w_o, ffn_norm_g[i], pw, tiles)
    return h.reshape(batch, seq, d)
```

```python
import math

import jax
import jax.numpy as jnp
import numpy as np
from jax import lax
from jax.experimental import pallas as pl
from jax.experimental.pallas import tpu as pltpu

F32 = jnp.float32
BF16 = jnp.bfloat16
HIGHEST = lax.Precision.HIGHEST

LANES = 128
SUBLANES = 8
NORM_EPS = 1e-6
VMEM_LIMIT = 56 * 1024 * 1024

MLA_HEADS = 16
MLA_Q_RANK = 384
MLA_KV_RANK = 256
MLA_NOPE = 64
MLA_ROPE = 32
MLA_QK = MLA_NOPE + MLA_ROPE
MLA_V = 64
ROPE_THETA = 10000.0
GLA_HEADS = 4
GLA_DK = 128
GLA_DV = 256
GLA_GATE_RANK = 16
GLA_TAU = 16.0
GLA_CHUNK = 64
PEER_HEADS = 8
PEER_NKEYS = 128
PEER_HALF = 64
PEER_TOPK = 16
PEER_RANK_ROWS = 24
NEG_INF = float("-inf")
PEER_STEP_KEYS = 8
ATTN_ROW_CHUNK = 1024


def _params(sem):
    return pltpu.CompilerParams(dimension_semantics=sem, vmem_limit_bytes=VMEM_LIMIT)


def _rms(x, g, n=None):
    n = x.shape[-1] if n is None else n
    ss = jnp.sum(x * x, axis=-1, keepdims=True)
    return x * lax.rsqrt(ss * (1.0 / n) + NORM_EPS) * g


def _split3(x):
    hi = x.astype(BF16)
    lo = (x - hi.astype(F32)).astype(BF16)
    return jnp.concatenate([hi, lo, hi], axis=-1)


def _split3_const(w, axis):
    bits = lax.bitcast_convert_type(w, jnp.uint32) & jnp.uint32(0xFFFF0000)
    hi = lax.bitcast_convert_type(bits, F32)
    lo = (w - hi).astype(BF16)
    hi = hi.astype(BF16)
    return jnp.concatenate([hi, hi, lo], axis=axis)


def _full(shape):
    nd = len(shape)
    return pl.BlockSpec(shape, lambda *_: (0,) * nd)


def _mla_proj_kernel(x_ref, pos_ref, gin_ref, wd_ref, gql_ref, wuq_ref, gkl_ref, wuk_ref, wuv_ref,
                     gqn_ref, gkn_ref, freq_ref, rotw_ref, vone_ref, q_ref, k_ref, v_ref):
    x = x_ref[...]
    hn = _rms(x, gin_ref[...])
    down = jnp.dot(hn.astype(BF16), wd_ref[...], preferred_element_type=F32)
    c_q = _rms(down[:, :MLA_Q_RANK], gql_ref[...]).astype(BF16)
    c_kv = _rms(down[:, MLA_Q_RANK:MLA_Q_RANK + MLA_KV_RANK], gkl_ref[...]).astype(BF16)
    kr = down[:, MLA_Q_RANK + MLA_KV_RANK:]
    qf = jnp.dot(c_q, wuq_ref[...], preferred_element_type=F32)
    kf = jnp.dot(c_kv, wuk_ref[...], preferred_element_type=F32)
    v_ref[...] = (jnp.dot(c_kv, wuv_ref[...], preferred_element_type=F32) + vone_ref[...]).astype(v_ref.dtype)

    ang = pos_ref[...] * freq_ref[...]
    cos = jnp.cos(ang)
    sin = jnp.sin(ang)

    def head(t, g):
        tg = t * g
        res = jnp.dot(jnp.concatenate([tg.astype(BF16), (t * t).astype(BF16)], axis=1), rotw_ref[...],
                      preferred_element_type=F32)
        r = lax.rsqrt(res[:, LANES:] * (1.0 / MLA_QK) + NORM_EPS)
        return r * (tg * cos + res[:, :LANES] * sin)

    for h in range(MLA_HEADS):
        sl = slice(h * LANES, (h + 1) * LANES)
        q_ref[:, sl] = head(qf[:, sl], gqn_ref[...]).astype(q_ref.dtype)
        k_ref[:, sl] = head(kf[:, sl] + kr, gkn_ref[...]).astype(k_ref.dtype)


def _mla_proj(x2, pos2, g_in, w, tm):
    n, d = x2.shape
    hq = MLA_HEADS * LANES
    row = lambda c: pl.BlockSpec((tm, c), lambda i: (i, 0))
    return pl.pallas_call(
        _mla_proj_kernel,
        grid=(n // tm,),
        in_specs=[row(d), row(1), _full((1, d)), _full(w["wd"].shape), _full((1, MLA_Q_RANK)),
                  _full(w["wuq"].shape), _full((1, MLA_KV_RANK)), _full(w["wuk"].shape),
                  _full(w["wuv"].shape), _full((1, LANES)), _full((1, LANES)), _full((1, LANES)),
                  _full(w["rotw"].shape), _full((1, hq))],
        out_specs=[row(hq), row(hq), row(hq)],
        out_shape=[jax.ShapeDtypeStruct((n, hq), BF16)] * 3,
        compiler_params=_params(("parallel",)),
        name="mla_proj",
    )(x2, pos2, g_in, w["wd"], w["gql"], w["wuq"], w["gkl"], w["wuk"], w["wuv"], w["gqn"], w["gkn"],
      w["freq"], w["rotw"], w["vone"])


def _attn_kernel(qi_ref, ki_ref, q_ref, k_ref, v_ref, o_ref, m_scr, acc_scr):
    p_id = pl.program_id(2)
    qi = qi_ref[p_id]
    ki = ki_ref[p_id]
    tq = q_ref.shape[0]
    tk = k_ref.shape[0]
    rq = min(ATTN_ROW_CHUNK, tq)

    @pl.when(ki == 0)
    def _():
        m_scr[...] = jnp.full(m_scr.shape, NEG_INF, F32)
        acc_scr[...] = jnp.zeros(acc_scr.shape, F32)

    def step(diagonal):
        for hh in range(2):
            sl = slice(hh * LANES, (hh + 1) * LANES)
            for r in range(tq // rq):
                rows = slice(r * rq, (r + 1) * rq)
                nk = (r + 1) * rq if diagonal else tk
                s = lax.dot_general(q_ref[rows, sl], k_ref[:nk, sl], (((1,), (1,)), ((), ())),
                                    preferred_element_type=F32)
                if diagonal:
                    row = lax.broadcasted_iota(jnp.int32, (rq, rq), 0)
                    col = lax.broadcasted_iota(jnp.int32, (rq, rq), 1)
                    tail = jnp.where(col <= row, s[:, r * rq:], NEG_INF)
                    s = tail if r == 0 else jnp.concatenate([s[:, :r * rq], tail], axis=1)
                m_prev = m_scr[hh, rows]
                m_new = jnp.maximum(m_prev, jnp.max(s, axis=-1, keepdims=True))
                alpha = jnp.exp2(m_prev - m_new)
                p = jnp.concatenate(
                    [jnp.exp2(s[:, c * LANES:(c + 1) * LANES] - m_new).astype(BF16)
                     for c in range(nk // LANES)], axis=1)
                pv = jnp.dot(p, v_ref[:nk, sl], preferred_element_type=F32)
                acc_scr[hh, rows] = alpha * acc_scr[hh, rows] + pv
                m_scr[hh, rows] = m_new

    @pl.when(ki < qi)
    def _():
        step(False)

    @pl.when(ki == qi)
    def _():
        step(True)
        lane = lax.broadcasted_iota(jnp.int32, (tq, LANES), 1)
        a0 = acc_scr[0]
        a1 = acc_scr[1]
        o0 = a0 / pltpu.roll(a0, MLA_V, 1)
        o1 = pltpu.roll(a1, MLA_V, 1) / a1
        o_ref[...] = jnp.where(lane < MLA_V, o0, o1).astype(o_ref.dtype)


def _attention(q, k, v, batch, seq, tq):
    nq = seq // tq
    qi_tab = np.array([qi for qi in range(nq) for _ in range(qi + 1)], np.int32)
    ki_tab = np.array([ki for qi in range(nq) for ki in range(qi + 1)], np.int32)
    npairs = len(qi_tab)
    hp = MLA_HEADS // 2
    blk = lambda tab: pl.BlockSpec((tq, 2 * LANES), lambda b, h, p, qt, kt: (b * nq + (qt, kt)[tab][p], h))
    grid_spec = pltpu.PrefetchScalarGridSpec(
        num_scalar_prefetch=2,
        grid=(batch, hp, npairs),
        in_specs=[blk(0), blk(1), blk(1)],
        out_specs=pl.BlockSpec((tq, 2 * MLA_V), lambda b, h, p, qt, kt: (b * nq + qt[p], h)),
        scratch_shapes=[pltpu.VMEM((2, tq, LANES), F32), pltpu.VMEM((2, tq, LANES), F32)],
    )
    return pl.pallas_call(
        _attn_kernel,
        grid_spec=grid_spec,
        out_shape=jax.ShapeDtypeStruct((batch * seq, MLA_HEADS * MLA_V), BF16),
        compiler_params=_params(("parallel", "parallel", "arbitrary")),
        name="mla_attention",
    )(jnp.asarray(qi_tab), jnp.asarray(ki_tab), q, k, v)


def _sort_network(n):
    pairs = []
    p = 1
    while p < n:
        k = p
        while k >= 1:
            for j in range(k % p, n - k, 2 * k):
                for i in range(min(k, n - j - k)):
                    if (i + j) // (2 * p) == (i + j + k) // (2 * p):
                        pairs.append((i + j, i + j + k))
            k //= 2
        p *= 2
    return pairs


def _sorted_top(cur, n, val_scr, idx_scr=None):
    sub = SUBLANES
    groups = cur.shape[0] // sub
    track = idx_scr is not None
    base = lax.broadcasted_iota(jnp.int32, (sub, LANES), 0).astype(F32)
    network = _sort_network(groups)
    for c in range(cur.shape[1] // LANES):
        csl = slice(c * LANES, (c + 1) * LANES)
        vals = [cur[g * sub:(g + 1) * sub, csl] for g in range(groups)]
        if track:
            idxs = [base + float(g * sub) for g in range(groups)]
        for a, b in network:
            if track:
                keep = vals[a] >= vals[b]
                idxs[a], idxs[b] = jnp.where(keep, idxs[a], idxs[b]), jnp.where(keep, idxs[b], idxs[a])
            vals[a], vals[b] = jnp.maximum(vals[a], vals[b]), jnp.minimum(vals[a], vals[b])
        for r in range(n):
            m = jnp.max(vals[0], axis=0, keepdims=True)
            val_scr[r:r + 1, csl] = m
            hit = vals[0] == m
            if track:
                first = jnp.min(jnp.where(hit, base, float(sub)), axis=0, keepdims=True)
                hit = base == first
                idx_scr[r:r + 1, csl] = jnp.max(jnp.where(hit, idxs[0], -1.0), axis=0, keepdims=True)
            depth = min(n - 1 - r, groups)
            for k in range(depth):
                below = vals[k + 1] if k + 1 < groups else NEG_INF
                vals[k] = jnp.where(hit, below, vals[k])
                if track and k + 1 < groups:
                    idxs[k] = jnp.where(hit, idxs[k + 1], idxs[k])


def _peer_route_kernel(res_ref, a_ref, wo_ref, g_ref, wq_ref, k0_ref, k1_ref,
                       h_ref, x_ref, idx_ref, a2_ref, th_ref, s1_ref, b2_ref,
                       a_scr, b_scr, c_scr, i_scr, idx_all, a2_all):
    tr = res_ref.shape[0]
    h = res_ref[...] + jnp.dot(a_ref[...], wo_ref[...], preferred_element_type=F32)
    h_ref[...] = h
    hn = _rms(h, g_ref[...])
    x_ref[...] = (hn * (1.0 / math.sqrt(2.0))).astype(x_ref.dtype)
    q = jnp.dot(_split3(hn), wq_ref[...], preferred_element_type=F32)
    q3 = [_split3(q[:, hd * LANES:(hd + 1) * LANES]) for hd in range(PEER_HEADS)]
    nk = PEER_TOPK + 1
    rows = lax.broadcasted_iota(jnp.int32, (SUBLANES, tr), 0)
    a_scr[...] = jnp.full(a_scr.shape, NEG_INF, F32)
    b_scr[...] = jnp.full(b_scr.shape, NEG_INF, F32)
    contract_last = (((1,), (1,)), ((), ()))
    log2e = math.log2(math.e)
    for hd in range(PEER_HEADS):
        s0 = lax.dot_general(k0_ref[...], q3[hd], contract_last, preferred_element_type=F32)
        s1 = lax.dot_general(k1_ref[...], q3[hd], contract_last, preferred_element_type=F32)
        _sorted_top(s0, nk, a_scr, i_scr)
        _sorted_top(s1, nk, b_scr)
        pieces = [b_scr[...] + a_scr[0:1, :]]
        r = 1
        while nk // (r + 1) >= 2:
            pieces.append(jnp.where(rows < nk // (r + 1), b_scr[0:SUBLANES, :] + a_scr[r:r + 1, :], NEG_INF))
            r += 1
        assert r == SUBLANES
        pieces.append(a_scr[SUBLANES:PEER_RANK_ROWS, :] + b_scr[0:1, :])
        cand = jnp.concatenate(pieces, axis=0)
        padded = jnp.concatenate([cand, jnp.full((PEER_NKEYS - cand.shape[0], tr), NEG_INF, F32)], axis=0)
        _sorted_top(padded, nk, c_scr)
        best = c_scr[0:1, :]
        tau = 0.5 * (c_scr[PEER_TOPK - 1:PEER_TOPK, :] + c_scr[PEER_TOPK:PEER_TOPK + 1, :])
        z = jnp.sum(jnp.where(cand >= tau, jnp.exp(cand - best), 0.0), axis=0, keepdims=True)
        a_max = a_scr[0:1, :]
        a_rel = (a_scr[0:PEER_TOPK, :] - a_max) * log2e
        s1_rel = (s1 - (best - a_max + jnp.log(z) + 0.5 * math.log(2.0))) * log2e
        thr = (tau - (best + jnp.log(z) + 0.5 * math.log(2.0))) * log2e
        slots = slice(hd * PEER_TOPK, (hd + 1) * PEER_TOPK)
        idx_all[slots, :] = i_scr[0:PEER_TOPK, :]
        a2_all[slots, :] = jnp.exp2(a_rel)
        th_ref[slots, :] = thr - a_rel
        s1t = s1_rel.T
        s1_ref[hd] = s1t
        b2_ref[hd] = jnp.exp2(s1t)
    idx_ref[...] = idx_all[...].T
    a2_ref[...] = a2_all[...].T


def _peer_route(res, a, w_o, g, wq, k0, k1, tr):
    n, d = res.shape
    da = a.shape[1]
    nh = PEER_HEADS
    ns = nh * PEER_TOPK
    row = lambda c: pl.BlockSpec((tr, c), lambda i: (i, 0))
    per_head = pl.BlockSpec((nh, tr, PEER_NKEYS), lambda i: (0, i, 0))
    return pl.pallas_call(
        _peer_route_kernel,
        grid=(n // tr,),
        in_specs=[row(d), row(da), _full(w_o.shape), _full((1, d)), _full(wq.shape), _full(k0.shape),
                  _full(k1.shape)],
        out_specs=[row(d), row(d), row(ns), row(ns), pl.BlockSpec((ns, tr), lambda i: (0, i)),
                   per_head, per_head],
        out_shape=[jax.ShapeDtypeStruct((n, d), F32), jax.ShapeDtypeStruct((n, d), BF16),
                   jax.ShapeDtypeStruct((n, ns), F32), jax.ShapeDtypeStruct((n, ns), F32),
                   jax.ShapeDtypeStruct((ns, n), F32),
                   jax.ShapeDtypeStruct((nh, n, PEER_NKEYS), F32),
                   jax.ShapeDtypeStruct((nh, n, PEER_NKEYS), F32)],
        scratch_shapes=[pltpu.VMEM((PEER_RANK_ROWS, tr), F32)] * 4 + [pltpu.VMEM((ns, tr), F32)] * 2,
        compiler_params=_params(("parallel",)),
        name="peer_route",
    )(res, a, w_o, g, wq, k0, k1)


def _peer_weights_kernel(idx_ref, a2_ref, th_ref, s1_ref, b2_ref, g_ref):
    tg = idx_ref.shape[0]
    ns = th_ref.shape[0]
    sub = lax.broadcasted_iota(jnp.int32, (PEER_NKEYS, ns), 0).astype(F32)

    def head_rows(ref, t):
        return jnp.concatenate([jnp.broadcast_to(ref[hd, t:t + 1, :], (PEER_TOPK, PEER_NKEYS))
                                for hd in range(PEER_HEADS)], axis=0)

    for t in range(tg):
        first = jnp.where(sub == idx_ref[t:t + 1, :], a2_ref[t:t + 1, :], 0.0).astype(BF16)
        thr = jnp.broadcast_to(th_ref[:, t:t + 1], (ns, PEER_NKEYS))
        second = jnp.where(head_rows(s1_ref, t) >= thr, head_rows(b2_ref, t), 0.0).astype(BF16)
        gt = jnp.dot(first, second, preferred_element_type=F32)
        for blk in range(PEER_NKEYS // PEER_STEP_KEYS):
            g_ref[blk, t] = gt[blk * PEER_STEP_KEYS:(blk + 1) * PEER_STEP_KEYS, :]


def _peer_weights(idx, a2, th, s1, b2, tg):
    n, ns = idx.shape
    nh = PEER_HEADS
    nb = PEER_NKEYS // PEER_STEP_KEYS
    row = pl.BlockSpec((tg, ns), lambda i: (i, 0))
    per_head = pl.BlockSpec((nh, tg, PEER_NKEYS), lambda i: (0, i, 0))
    return pl.pallas_call(
        _peer_weights_kernel,
        grid=(n // tg,),
        in_specs=[row, row, pl.BlockSpec((ns, tg), lambda i: (0, i)), per_head, per_head],
        out_specs=pl.BlockSpec((nb, tg, PEER_STEP_KEYS, PEER_NKEYS), lambda i: (0, i, 0, 0)),
        out_shape=jax.ShapeDtypeStruct((nb, n, PEER_STEP_KEYS, PEER_NKEYS), F32),
        compiler_params=_params(("parallel",)),
        name="peer_weights",
    )(idx, a2, th, s1, b2)


def _peer_dense_kernel(x_ref, g_ref, u_ref, v_ref, h_ref, o_ref, acc_scr):
    e = pl.program_id(1)
    ne = pl.num_programs(1)
    t = x_ref.shape[0]
    rows_per_step = PEER_STEP_KEYS

    @pl.when(e == 0)
    def _():
        acc_scr[...] = jnp.zeros(acc_scr.shape, F32)

    y = lax.dot_general(x_ref[...], u_ref[...], (((1,), (1,)), ((), ())),
                        preferred_element_type=F32)
    ps = []
    for ii in range(rows_per_step):
        yc = y[:, ii * PEER_NKEYS:(ii + 1) * PEER_NKEYS]
        gi = g_ref[pl.ds(ii, t, stride=rows_per_step), :]
        ps.append((gi * (yc * (1.0 + lax.erf(yc)))).astype(BF16))
    acc_scr[...] += jnp.dot(jnp.concatenate(ps, axis=1), v_ref[...], preferred_element_type=F32)

    @pl.when(e == ne - 1)
    def _():
        o_ref[...] = h_ref[...] + acc_scr[...]


def _peer_dense(x, g, u, v, h, t, eb):
    n, d = x.shape
    ne = u.shape[0] // eb
    tok = lambda c: pl.BlockSpec((t, c), lambda i, e: (i, 0))
    return pl.pallas_call(
        _peer_dense_kernel,
        grid=(n // t, ne),
        in_specs=[tok(d),
                  pl.BlockSpec((None, t * PEER_STEP_KEYS, PEER_NKEYS), lambda i, e: (e, i, 0)),
                  pl.BlockSpec((eb, d), lambda i, e: (e, 0)),
                  pl.BlockSpec((eb, d), lambda i, e: (e, 0)),
                  pl.BlockSpec((t, d), lambda i, e: (i, 0), pipeline_mode=pl.Buffered(1))],
        out_specs=tok(d),
        out_shape=jax.ShapeDtypeStruct((n, d), F32),
        scratch_shapes=[pltpu.VMEM((t, d), F32)],
        compiler_params=_params(("parallel", "arbitrary")),
        name="peer_dense",
    )(x, g, u, v, h)


def _gla_kernel(x_ref, gin_ref, win_ref, wg2_ref, bg_ref, gon_ref, tri_ref, o_ref, state_scr):
    tc = x_ref.shape[0]
    c_len = GLA_CHUNK
    n_chunks = tc // c_len
    hk = GLA_HEADS * GLA_DK
    hv = GLA_HEADS * GLA_DV

    @pl.when(pl.program_id(1) == 0)
    def _():
        state_scr[...] = jnp.zeros(state_scr.shape, F32)

    hn = _rms(x_ref[...], gin_ref[...])
    proj = jnp.dot(hn.astype(BF16), win_ref[...], preferred_element_type=F32)
    q = proj[:, :hk] * (GLA_DK ** -0.5)
    k = proj[:, hk:2 * hk]
    g_low = proj[:, 2 * hk + 2 * hv:]
    z = jnp.dot(g_low, wg2_ref[...], precision=HIGHEST, preferred_element_type=F32) + bg_ref[...]
    log_a = (jnp.minimum(z, 0.0) - jnp.log1p(jnp.exp(-jnp.abs(z)))) * (1.0 / GLA_TAU)

    hi = log_a.astype(BF16)
    rest = log_a - hi.astype(F32)
    mid = rest.astype(BF16)
    lo = (rest - mid.astype(F32)).astype(BF16)
    cum = jnp.dot(tri_ref[...], jnp.concatenate([hi, mid, lo], axis=1), preferred_element_type=F32)
    b = cum[:, :hk] + cum[:, hk:2 * hk] + cum[:, 2 * hk:]
    b_last = jnp.concatenate(
        [jnp.broadcast_to(b[(c + 1) * c_len - 1:(c + 1) * c_len, :], (c_len, hk)) for c in range(n_chunks)],
        axis=0)
    q_dec = (q * jnp.exp(b)).astype(BF16)
    k_inv = k * jnp.exp(-b)
    dec = jnp.exp(b_last)
    k_rem = k_inv * dec
    k_inv = k_inv.astype(BF16)

    row_chunk = lax.broadcasted_iota(jnp.int32, (tc, GLA_DK), 0) // c_len
    col_chunk = lax.broadcasted_iota(jnp.int32, (GLA_DK, tc), 1) // c_len
    causal = tri_ref[...] > 0
    contract_last = (((1,), (1,)), ((), ()))
    for h in range(GLA_HEADS):
        ksl = slice(h * GLA_DK, (h + 1) * GLA_DK)
        vsl = slice(2 * hk + h * GLA_DV, 2 * hk + (h + 1) * GLA_DV)
        rsl = slice(2 * hk + hv + h * GLA_DV, 2 * hk + hv + (h + 1) * GLA_DV)
        qd = q_dec[:, ksl]
        vh = proj[:, vsl].astype(BF16)
        attn = lax.dot_general(qd, k_inv[:, ksl], contract_last, preferred_element_type=F32)
        o = jnp.dot(jnp.where(causal, attn, 0.0).astype(BF16), vh, preferred_element_type=F32)
        k_t = k_rem[:, ksl].T.astype(BF16)
        kv = jnp.dot(jnp.concatenate([jnp.where(col_chunk == c, k_t, 0.0) for c in range(n_chunks)], axis=0),
                     vh, preferred_element_type=F32)
        state = state_scr[h]
        states = []
        for c in range(n_chunks):
            states.append(state.astype(BF16))
            dec_col = jnp.broadcast_to(dec[c * c_len:c * c_len + 1, ksl], (8, GLA_DK)).T[:, 0:1]
            state = state * dec_col + kv[c * GLA_DK:(c + 1) * GLA_DK, :]
        state_scr[h] = state
        o = o + jnp.dot(jnp.concatenate([jnp.where(row_chunk == c, qd, 0.0) for c in range(n_chunks)], axis=1),
                        jnp.concatenate(states, axis=0), preferred_element_type=F32)
        rr = proj[:, rsl]
        o_ref[:, h * GLA_DV:(h + 1) * GLA_DV] = (_rms(o, gon_ref[...]) * (rr * jax.nn.sigmoid(rr))).astype(o_ref.dtype)


def _gla(x2, g_in, w, g_on, batch, seq, tc):
    n, d = x2.shape
    hk = GLA_HEADS * GLA_DK
    hv = GLA_HEADS * GLA_DV
    ns = seq // tc
    idx = np.arange(tc)
    tri = ((idx[:, None] // GLA_CHUNK == idx[None, :] // GLA_CHUNK) & (idx[None, :] <= idx[:, None]))
    return pl.pallas_call(
        _gla_kernel,
        grid=(batch, ns),
        in_specs=[pl.BlockSpec((tc, d), lambda b, s: (b * ns + s, 0)), _full((1, d)), _full(w["win"].shape),
                  _full(w["wg2"].shape), _full((1, hk)), _full((1, GLA_DV)), _full((tc, tc))],
        out_specs=pl.BlockSpec((tc, hv), lambda b, s: (b * ns + s, 0)),
        out_shape=jax.ShapeDtypeStruct((n, hv), BF16),
        scratch_shapes=[pltpu.VMEM((GLA_HEADS, GLA_DK, GLA_DV), F32)],
        compiler_params=_params(("parallel", "arbitrary")),
        name="gla",
    )(x2, g_in, w["win"], w["wg2"], w["bg"], g_on, jnp.asarray(tri, BF16))


def _prep_mla(w_down, g_q_lat, w_uq, g_kv_lat, w_ukv, g_qn, g_kn):
    d = w_down.shape[0]
    lat = MLA_Q_RANK + MLA_KV_RANK
    z = lambda c: jnp.zeros((d, c), F32)
    wd = jnp.concatenate([w_down[:, :lat], z(MLA_NOPE), w_down[:, lat:], z(LANES - MLA_QK)], axis=1)
    pad_head = LANES - MLA_QK
    wuq = jnp.pad(w_uq.reshape(MLA_Q_RANK, MLA_HEADS, MLA_QK), ((0, 0), (0, 0), (0, pad_head)))
    wukv = w_ukv.reshape(MLA_KV_RANK, MLA_HEADS, MLA_NOPE + MLA_V)
    wuk = jnp.pad(wukv[:, :, :MLA_NOPE], ((0, 0), (0, 0), (0, LANES - MLA_NOPE)))
    wuv = jnp.pad(wukv[:, :, MLA_NOPE:], ((0, 0), (0, 0), (0, LANES - MLA_V)))
    half = MLA_ROPE // 2
    inv_freq = ROPE_THETA ** (-jnp.arange(half, dtype=F32) / half)
    zero = lambda c: jnp.zeros((c,), F32)
    vone = jnp.tile(jnp.concatenate([zero(MLA_V), jnp.ones((LANES - MLA_V,), F32)]), MLA_HEADS)
    freq = jnp.concatenate([zero(MLA_NOPE), inv_freq, inv_freq, zero(pad_head)])
    rotw = np.zeros((2 * LANES, 2 * LANES), np.float32)
    for lane in range(MLA_NOPE, MLA_NOPE + half):
        rotw[lane + half, lane] = -1.0
        rotw[lane, lane + half] = 1.0
    rotw[LANES:, LANES:] = 1.0
    scale = MLA_QK ** -0.5 * math.log2(math.e)
    return {
        "wd": wd.astype(BF16),
        "gql": g_q_lat.reshape(1, -1), "gkl": g_kv_lat.reshape(1, -1),
        "wuq": wuq.reshape(MLA_Q_RANK, MLA_HEADS * LANES).astype(BF16),
        "wuk": wuk.reshape(MLA_KV_RANK, MLA_HEADS * LANES).astype(BF16),
        "wuv": wuv.reshape(MLA_KV_RANK, MLA_HEADS * LANES).astype(BF16),
        "vone": vone.reshape(1, MLA_HEADS * LANES),
        "gqn": (jnp.pad(g_qn, (0, pad_head)) * scale).reshape(1, LANES),
        "gkn": jnp.pad(g_kn, (0, pad_head)).reshape(1, LANES),
        "freq": freq.reshape(1, LANES), "rotw": jnp.asarray(rotw, BF16),
    }


def _prep_gla(w_in, w_g2, b_g):
    pad = LANES - GLA_GATE_RANK
    return {
        "win": jnp.pad(w_in, ((0, 0), (0, pad))).astype(BF16),
        "wg2": jnp.pad(w_g2, ((0, pad), (0, 0))),
        "bg": b_g.reshape(1, -1),
    }


def _prep_peer(w_query, sub_keys, u_tab, v_tab):
    zk = jnp.zeros((PEER_NKEYS, PEER_HALF), F32)
    return {
        "wq": _split3_const(w_query, 0),
        "k0": _split3_const(jnp.concatenate([sub_keys[0], zk], axis=1), 1),
        "k1": _split3_const(jnp.concatenate([zk, sub_keys[1]], axis=1), 1),
        "u": u_tab.astype(BF16),
        "v": v_tab.astype(BF16),
    }


def _peer_layer(res, a, w_o, g, pw, tiles):
    h, xn, idx, a2, th, s1, b2 = _peer_route(res, a, w_o, g.reshape(1, -1), pw["wq"], pw["k0"], pw["k1"],
                                             tiles["route"])
    gate = _peer_weights(idx, a2, th, s1, b2, tiles["weights"])
    gate = gate.reshape(gate.shape[0], -1, PEER_NKEYS)
    return _peer_dense(xn, gate, pw["u"], pw["v"], h, tiles["dense_t"], PEER_STEP_KEYS * PEER_NKEYS)


def _tiles(seq):
    pick = lambda want: math.gcd(want, seq)
    return {"proj": pick(512), "attn": pick(2048), "route": pick(512), "weights": pick(128),
            "dense_t": pick(1024), "gla": pick(256)}


def kernel(x, positions, attn_norm_g, ffn_norm_g, mla_w_down, mla_g_q_lat, mla_w_uq, mla_g_kv_lat,
           mla_w_ukv, mla_g_qn, mla_g_kn, mla_w_o, gla_w_in, gla_w_g2, gla_b_g, gla_g_on, gla_w_o,
           peer_w_query, peer_sub_keys, peer_u, peer_v):
    batch, seq, d = x.shape
    n = batch * seq
    depth = attn_norm_g.shape[0]
    tiles = _tiles(seq)
    h = x.reshape(n, d)
    pos = positions.reshape(n, 1).astype(F32)
    for i in range(depth):
        j = i // 2
        g_in = attn_norm_g[i].reshape(1, d)
        if i % 2 == 0:
            w = _prep_mla(mla_w_down[j], mla_g_q_lat[j], mla_w_uq[j], mla_g_kv_lat[j], mla_w_ukv[j],
                          mla_g_qn[j], mla_g_kn[j])
            q, k, v = _mla_proj(h, pos, g_in, w, tiles["proj"])
            a = _attention(q, k, v, batch, seq, tiles["attn"])
            w_o = mla_w_o[j].astype(BF16)
        else:
            w = _prep_gla(gla_w_in[j], gla_w_g2[j], gla_b_g[j])
            a = _gla(h, g_in, w, gla_g_on[j].reshape(1, -1), batch, seq, tiles["gla"])
            w_o = gla_w_o[j].astype(BF16)
        pw = _prep_peer(peer_w_query[i], peer_sub_keys[i], peer_u[i], peer_v[i])
        h = _peer_layer(h, a, w_o, ffn_norm_g[i], pw, tiles)
    return h.reshape(batch, seq, d)
```

```python
import math

import jax
import jax.numpy as jnp
import numpy as np
from jax import lax
from jax.experimental import pallas as pl
from jax.experimental.pallas import tpu as pltpu

F32 = jnp.float32
BF16 = jnp.bfloat16
HIGHEST = lax.Precision.HIGHEST

LANES = 128
SUBLANES = 8
NORM_EPS = 1e-6
VMEM_LIMIT = 56 * 1024 * 1024

MLA_HEADS = 16
MLA_Q_RANK = 384
MLA_KV_RANK = 256
MLA_NOPE = 64
MLA_ROPE = 32
MLA_QK = MLA_NOPE + MLA_ROPE
MLA_V = 64
ROPE_THETA = 10000.0
GLA_HEADS = 4
GLA_DK = 128
GLA_DV = 256
GLA_GATE_RANK = 16
GLA_TAU = 16.0
GLA_CHUNK = 64
GLA_BLOCK = 256
PEER_HEADS = 8
PEER_NKEYS = 128
PEER_HALF = 64
PEER_TOPK = 16
PEER_RANK_ROWS = 24
NEG_INF = float("-inf")
PEER_STEP_KEYS = 8
ATTN_ROW_CHUNK = 1024


def _params(sem):
    return pltpu.CompilerParams(dimension_semantics=sem, vmem_limit_bytes=VMEM_LIMIT)


def _rms(x, g, n=None):
    n = x.shape[-1] if n is None else n
    ss = jnp.sum(x * x, axis=-1, keepdims=True)
    return x * lax.rsqrt(ss * (1.0 / n) + NORM_EPS) * g


def _split3(x):
    hi = x.astype(BF16)
    lo = (x - hi.astype(F32)).astype(BF16)
    return jnp.concatenate([hi, lo, hi], axis=-1)


def _split3_const(w, axis):
    bits = lax.bitcast_convert_type(w, jnp.uint32) & jnp.uint32(0xFFFF0000)
    hi = lax.bitcast_convert_type(bits, F32)
    lo = (w - hi).astype(BF16)
    hi = hi.astype(BF16)
    return jnp.concatenate([hi, hi, lo], axis=axis)


def _full(shape):
    nd = len(shape)
    return pl.BlockSpec(shape, lambda *_: (0,) * nd)


def _mla_proj_kernel(x_ref, pos_ref, gin_ref, wd_ref, gql_ref, wuq_ref, gkl_ref, wuk_ref, wuv_ref,
                     gqn_ref, gkn_ref, freq_ref, rotw_ref, vone_ref, q_ref, k_ref, v_ref):
    x = x_ref[...]
    hn = _rms(x, gin_ref[...])
    down = jnp.dot(hn.astype(BF16), wd_ref[...], preferred_element_type=F32)
    c_q = _rms(down[:, :MLA_Q_RANK], gql_ref[...]).astype(BF16)
    c_kv = _rms(down[:, MLA_Q_RANK:MLA_Q_RANK + MLA_KV_RANK], gkl_ref[...]).astype(BF16)
    kr = down[:, MLA_Q_RANK + MLA_KV_RANK:]
    qf = jnp.dot(c_q, wuq_ref[...], preferred_element_type=F32)
    kf = jnp.dot(c_kv, wuk_ref[...], preferred_element_type=F32)
    v_ref[...] = (jnp.dot(c_kv, wuv_ref[...], preferred_element_type=F32) + vone_ref[...]).astype(v_ref.dtype)

    ang = pos_ref[...] * freq_ref[...]
    cos = jnp.cos(ang)
    sin = jnp.sin(ang)

    def head(t, g):
        tg = t * g
        res = jnp.dot(jnp.concatenate([tg.astype(BF16), (t * t).astype(BF16)], axis=1), rotw_ref[...],
                      preferred_element_type=F32)
        r = lax.rsqrt(res[:, LANES:] * (1.0 / MLA_QK) + NORM_EPS)
        return r * (tg * cos + res[:, :LANES] * sin)

    for h in range(MLA_HEADS):
        sl = slice(h * LANES, (h + 1) * LANES)
        q_ref[:, sl] = head(qf[:, sl], gqn_ref[...]).astype(q_ref.dtype)
        k_ref[:, sl] = head(kf[:, sl] + kr, gkn_ref[...]).astype(k_ref.dtype)


def _mla_proj(x2, pos2, g_in, w, tm):
    n, d = x2.shape
    hq = MLA_HEADS * LANES
    row = lambda c: pl.BlockSpec((tm, c), lambda i: (i, 0))
    return pl.pallas_call(
        _mla_proj_kernel,
        grid=(n // tm,),
        in_specs=[row(d), row(1), _full((1, d)), _full(w["wd"].shape), _full((1, MLA_Q_RANK)),
                  _full(w["wuq"].shape), _full((1, MLA_KV_RANK)), _full(w["wuk"].shape),
                  _full(w["wuv"].shape), _full((1, LANES)), _full((1, LANES)), _full((1, LANES)),
                  _full(w["rotw"].shape), _full((1, hq))],
        out_specs=[row(hq), row(hq), row(hq)],
        out_shape=[jax.ShapeDtypeStruct((n, hq), BF16)] * 3,
        compiler_params=_params(("parallel",)),
        name="mla_proj",
    )(x2, pos2, g_in, w["wd"], w["gql"], w["wuq"], w["gkl"], w["wuk"], w["wuv"], w["gqn"], w["gkn"],
      w["freq"], w["rotw"], w["vone"])


def _attn_kernel(qi_ref, ki_ref, q_ref, k_ref, v_ref, o_ref, m_scr, acc_scr):
    p_id = pl.program_id(2)
    qi = qi_ref[p_id]
    ki = ki_ref[p_id]
    tq = q_ref.shape[0]
    tk = k_ref.shape[0]
    rq = min(ATTN_ROW_CHUNK, tq)

    @pl.when(ki == 0)
    def _():
        m_scr[...] = jnp.full(m_scr.shape, NEG_INF, F32)
        acc_scr[...] = jnp.zeros(acc_scr.shape, F32)

    def step(diagonal):
        for hh in range(2):
            sl = slice(hh * LANES, (hh + 1) * LANES)
            for r in range(tq // rq):
                rows = slice(r * rq, (r + 1) * rq)
                nk = (r + 1) * rq if diagonal else tk
                s = lax.dot_general(q_ref[rows, sl], k_ref[:nk, sl], (((1,), (1,)), ((), ())),
                                    preferred_element_type=F32)
                if diagonal:
                    row = lax.broadcasted_iota(jnp.int32, (rq, rq), 0)
                    col = lax.broadcasted_iota(jnp.int32, (rq, rq), 1)
                    tail = jnp.where(col <= row, s[:, r * rq:], NEG_INF)
                    s = tail if r == 0 else jnp.concatenate([s[:, :r * rq], tail], axis=1)
                m_prev = m_scr[hh, rows]
                m_new = jnp.maximum(m_prev, jnp.max(s, axis=-1, keepdims=True))
                alpha = jnp.exp2(m_prev - m_new)
                p = jnp.concatenate(
                    [jnp.exp2(s[:, c * LANES:(c + 1) * LANES] - m_new).astype(BF16)
                     for c in range(nk // LANES)], axis=1)
                pv = jnp.dot(p, v_ref[:nk, sl], preferred_element_type=F32)
                acc_scr[hh, rows] = alpha * acc_scr[hh, rows] + pv
                m_scr[hh, rows] = m_new

    @pl.when(ki < qi)
    def _():
        step(False)

    @pl.when(ki == qi)
    def _():
        step(True)
        lane = lax.broadcasted_iota(jnp.int32, (tq, LANES), 1)
        a0 = acc_scr[0]
        a1 = acc_scr[1]
        o0 = a0 / pltpu.roll(a0, MLA_V, 1)
        o1 = pltpu.roll(a1, MLA_V, 1) / a1
        o_ref[...] = jnp.where(lane < MLA_V, o0, o1).astype(o_ref.dtype)


def _attention(q, k, v, batch, seq, tq):
    nq = seq // tq
    qi_tab = np.array([qi for qi in range(nq) for _ in range(qi + 1)], np.int32)
    ki_tab = np.array([ki for qi in range(nq) for ki in range(qi + 1)], np.int32)
    npairs = len(qi_tab)
    hp = MLA_HEADS // 2
    blk = lambda tab: pl.BlockSpec((tq, 2 * LANES), lambda b, h, p, qt, kt: (b * nq + (qt, kt)[tab][p], h))
    grid_spec = pltpu.PrefetchScalarGridSpec(
        num_scalar_prefetch=2,
        grid=(batch, hp, npairs),
        in_specs=[blk(0), blk(1), blk(1)],
        out_specs=pl.BlockSpec((tq, 2 * MLA_V), lambda b, h, p, qt, kt: (b * nq + qt[p], h)),
        scratch_shapes=[pltpu.VMEM((2, tq, LANES), F32), pltpu.VMEM((2, tq, LANES), F32)],
    )
    return pl.pallas_call(
        _attn_kernel,
        grid_spec=grid_spec,
        out_shape=jax.ShapeDtypeStruct((batch * seq, MLA_HEADS * MLA_V), BF16),
        compiler_params=_params(("parallel", "parallel", "arbitrary")),
        name="mla_attention",
    )(jnp.asarray(qi_tab), jnp.asarray(ki_tab), q, k, v)


def _sort_network(n):
    pairs = []
    p = 1
    while p < n:
        k = p
        while k >= 1:
            for j in range(k % p, n - k, 2 * k):
                for i in range(min(k, n - j - k)):
                    if (i + j) // (2 * p) == (i + j + k) // (2 * p):
                        pairs.append((i + j, i + j + k))
            k //= 2
        p *= 2
    return pairs


def _sorted_top(cur, n, val_scr, idx_scr=None):
    sub = SUBLANES
    groups = cur.shape[0] // sub
    track = idx_scr is not None
    base = lax.broadcasted_iota(jnp.int32, (sub, LANES), 0).astype(F32)
    network = _sort_network(groups)
    for c in range(cur.shape[1] // LANES):
        csl = slice(c * LANES, (c + 1) * LANES)
        vals = [cur[g * sub:(g + 1) * sub, csl] for g in range(groups)]
        if track:
            idxs = [base + float(g * sub) for g in range(groups)]
        for a, b in network:
            if track:
                keep = vals[a] >= vals[b]
                idxs[a], idxs[b] = jnp.where(keep, idxs[a], idxs[b]), jnp.where(keep, idxs[b], idxs[a])
            vals[a], vals[b] = jnp.maximum(vals[a], vals[b]), jnp.minimum(vals[a], vals[b])
        for r in range(n):
            m = jnp.max(vals[0], axis=0, keepdims=True)
            val_scr[r:r + 1, csl] = m
            hit = vals[0] == m
            if track:
                first = jnp.min(jnp.where(hit, base, float(sub)), axis=0, keepdims=True)
                hit = base == first
                idx_scr[r:r + 1, csl] = jnp.max(jnp.where(hit, idxs[0], -1.0), axis=0, keepdims=True)
            depth = min(n - 1 - r, groups)
            for k in range(depth):
                below = vals[k + 1] if k + 1 < groups else NEG_INF
                vals[k] = jnp.where(hit, below, vals[k])
                if track and k + 1 < groups:
                    idxs[k] = jnp.where(hit, idxs[k + 1], idxs[k])


def _peer_route_kernel(res_ref, a_ref, wo_ref, g_ref, wq_ref, k0_ref, k1_ref,
                       h_ref, x_ref, idx_ref, a2_ref, th_ref, s1_ref, b2_ref,
                       a_scr, b_scr, c_scr, i_scr, idx_all, a2_all):
    tr = res_ref.shape[0]
    h = res_ref[...] + jnp.dot(a_ref[...], wo_ref[...], preferred_element_type=F32)
    h_ref[...] = h
    hn = _rms(h, g_ref[...])
    x_ref[...] = (hn * (1.0 / math.sqrt(2.0))).astype(x_ref.dtype)
    q = jnp.dot(_split3(hn), wq_ref[...], preferred_element_type=F32)
    q3 = [_split3(q[:, hd * LANES:(hd + 1) * LANES]) for hd in range(PEER_HEADS)]
    nk = PEER_TOPK + 1
    rows = lax.broadcasted_iota(jnp.int32, (SUBLANES, tr), 0)
    a_scr[...] = jnp.full(a_scr.shape, NEG_INF, F32)
    b_scr[...] = jnp.full(b_scr.shape, NEG_INF, F32)
    contract_last = (((1,), (1,)), ((), ()))
    log2e = math.log2(math.e)
    for hd in range(PEER_HEADS):
        s0 = lax.dot_general(k0_ref[...], q3[hd], contract_last, preferred_element_type=F32)
        s1 = lax.dot_general(k1_ref[...], q3[hd], contract_last, preferred_element_type=F32)
        _sorted_top(s0, nk, a_scr, i_scr)
        _sorted_top(s1, nk, b_scr)
        pieces = [b_scr[...] + a_scr[0:1, :]]
        r = 1
        while nk // (r + 1) >= 2:
            pieces.append(jnp.where(rows < nk // (r + 1), b_scr[0:SUBLANES, :] + a_scr[r:r + 1, :], NEG_INF))
            r += 1
        assert r == SUBLANES
        pieces.append(a_scr[SUBLANES:PEER_RANK_ROWS, :] + b_scr[0:1, :])
        cand = jnp.concatenate(pieces, axis=0)
        padded = jnp.concatenate([cand, jnp.full((PEER_NKEYS - cand.shape[0], tr), NEG_INF, F32)], axis=0)
        _sorted_top(padded, nk, c_scr)
        best = c_scr[0:1, :]
        tau = 0.5 * (c_scr[PEER_TOPK - 1:PEER_TOPK, :] + c_scr[PEER_TOPK:PEER_TOPK + 1, :])
        z = jnp.sum(jnp.where(cand >= tau, jnp.exp(cand - best), 0.0), axis=0, keepdims=True)
        a_max = a_scr[0:1, :]
        a_rel = (a_scr[0:PEER_TOPK, :] - a_max) * log2e
        s1_rel = (s1 - (best - a_max + jnp.log(z) + 0.5 * math.log(2.0))) * log2e
        thr = (tau - (best + jnp.log(z) + 0.5 * math.log(2.0))) * log2e
        slots = slice(hd * PEER_TOPK, (hd + 1) * PEER_TOPK)
        idx_all[slots, :] = i_scr[0:PEER_TOPK, :]
        a2_all[slots, :] = jnp.exp2(a_rel)
        th_ref[slots, :] = thr - a_rel
        s1t = s1_rel.T
        s1_ref[hd] = s1t
        b2_ref[hd] = jnp.exp2(s1t)
    idx_ref[...] = idx_all[...].T
    a2_ref[...] = a2_all[...].T


def _peer_route(res, a, w_o, g, wq, k0, k1, tr):
    n, d = res.shape
    da = a.shape[1]
    nh = PEER_HEADS
    ns = nh * PEER_TOPK
    row = lambda c: pl.BlockSpec((tr, c), lambda i: (i, 0))
    per_head = pl.BlockSpec((nh, tr, PEER_NKEYS), lambda i: (0, i, 0))
    return pl.pallas_call(
        _peer_route_kernel,
        grid=(n // tr,),
        in_specs=[row(d), row(da), _full(w_o.shape), _full((1, d)), _full(wq.shape), _full(k0.shape),
                  _full(k1.shape)],
        out_specs=[row(d), row(d), row(ns), row(ns), pl.BlockSpec((ns, tr), lambda i: (0, i)),
                   per_head, per_head],
        out_shape=[jax.ShapeDtypeStruct((n, d), F32), jax.ShapeDtypeStruct((n, d), BF16),
                   jax.ShapeDtypeStruct((n, ns), F32), jax.ShapeDtypeStruct((n, ns), F32),
                   jax.ShapeDtypeStruct((ns, n), F32),
                   jax.ShapeDtypeStruct((nh, n, PEER_NKEYS), F32),
                   jax.ShapeDtypeStruct((nh, n, PEER_NKEYS), F32)],
        scratch_shapes=[pltpu.VMEM((PEER_RANK_ROWS, tr), F32)] * 4 + [pltpu.VMEM((ns, tr), F32)] * 2,
        compiler_params=_params(("parallel",)),
        name="peer_route",
    )(res, a, w_o, g, wq, k0, k1)


def _peer_weights_kernel(idx_ref, a2_ref, th_ref, s1_ref, b2_ref, g_ref):
    tg = idx_ref.shape[0]
    ns = th_ref.shape[0]
    sub = lax.broadcasted_iota(jnp.int32, (PEER_NKEYS, ns), 0).astype(F32)

    def head_rows(ref, t):
        return jnp.concatenate([jnp.broadcast_to(ref[hd, t:t + 1, :], (PEER_TOPK, PEER_NKEYS))
                                for hd in range(PEER_HEADS)], axis=0)

    for t in range(tg):
        first = jnp.where(sub == idx_ref[t:t + 1, :], a2_ref[t:t + 1, :], 0.0).astype(BF16)
        thr = jnp.broadcast_to(th_ref[:, t:t + 1], (ns, PEER_NKEYS))
        second = jnp.where(head_rows(s1_ref, t) >= thr, head_rows(b2_ref, t), 0.0).astype(BF16)
        gt = jnp.dot(first, second, preferred_element_type=F32)
        for blk in range(PEER_NKEYS // PEER_STEP_KEYS):
            g_ref[blk, t] = gt[blk * PEER_STEP_KEYS:(blk + 1) * PEER_STEP_KEYS, :]


def _peer_weights(idx, a2, th, s1, b2, tg):
    n, ns = idx.shape
    nh = PEER_HEADS
    nb = PEER_NKEYS // PEER_STEP_KEYS
    row = pl.BlockSpec((tg, ns), lambda i: (i, 0))
    per_head = pl.BlockSpec((nh, tg, PEER_NKEYS), lambda i: (0, i, 0))
    return pl.pallas_call(
        _peer_weights_kernel,
        grid=(n // tg,),
        in_specs=[row, row, pl.BlockSpec((ns, tg), lambda i: (0, i)), per_head, per_head],
        out_specs=pl.BlockSpec((nb, tg, PEER_STEP_KEYS, PEER_NKEYS), lambda i: (0, i, 0, 0)),
        out_shape=jax.ShapeDtypeStruct((nb, n, PEER_STEP_KEYS, PEER_NKEYS), F32),
        compiler_params=_params(("parallel",)),
        name="peer_weights",
    )(idx, a2, th, s1, b2)


def _peer_dense_kernel(x_ref, g_ref, u_ref, v_ref, h_ref, o_ref, acc_scr):
    e = pl.program_id(1)
    ne = pl.num_programs(1)
    t = x_ref.shape[0]
    rows_per_step = PEER_STEP_KEYS

    @pl.when(e == 0)
    def _():
        acc_scr[...] = jnp.zeros(acc_scr.shape, F32)

    y = lax.dot_general(x_ref[...], u_ref[...], (((1,), (1,)), ((), ())),
                        preferred_element_type=F32)
    ps = []
    for ii in range(rows_per_step):
        yc = y[:, ii * PEER_NKEYS:(ii + 1) * PEER_NKEYS]
        gi = g_ref[pl.ds(ii, t, stride=rows_per_step), :]
        ps.append((gi * (yc * (1.0 + lax.erf(yc)))).astype(BF16))
    acc_scr[...] += jnp.dot(jnp.concatenate(ps, axis=1), v_ref[...], preferred_element_type=F32)

    @pl.when(e == ne - 1)
    def _():
        o_ref[...] = h_ref[...] + acc_scr[...]


def _peer_dense(x, g, u, v, layer, h, t, eb):
    n, d = x.shape
    ne = u.shape[1] // eb
    tok = lambda c: pl.BlockSpec((t, c), lambda i, e: (i, 0))
    return pl.pallas_call(
        _peer_dense_kernel,
        grid=(n // t, ne),
        in_specs=[tok(d),
                  pl.BlockSpec((None, t * PEER_STEP_KEYS, PEER_NKEYS), lambda i, e: (e, i, 0)),
                  pl.BlockSpec((None, eb, d), lambda i, e: (layer, e, 0)),
                  pl.BlockSpec((None, eb, d), lambda i, e: (layer, e, 0)),
                  pl.BlockSpec((t, d), lambda i, e: (i, 0), pipeline_mode=pl.Buffered(1))],
        out_specs=tok(d),
        out_shape=jax.ShapeDtypeStruct((n, d), F32),
        scratch_shapes=[pltpu.VMEM((t, d), F32)],
        compiler_params=_params(("parallel", "arbitrary")),
        name="peer_dense",
    )(x, g, u, v, h)


def _gla_kernel(x_ref, gin_ref, win_ref, wg2_ref, bg_ref, gon_ref, tri_ref, o_ref, state_scr):
    tc = x_ref.shape[0]
    tb = tri_ref.shape[0]
    c_len = GLA_CHUNK
    n_chunks = tb // c_len
    hk = GLA_HEADS * GLA_DK
    hv = GLA_HEADS * GLA_DV

    @pl.when(pl.program_id(1) == 0)
    def _():
        state_scr[...] = jnp.zeros(state_scr.shape, F32)

    hn = _rms(x_ref[...], gin_ref[...])
    proj = jnp.dot(hn.astype(BF16), win_ref[...], preferred_element_type=F32)
    g_low = proj[:, 2 * hk + 2 * hv:]
    z = jnp.dot(g_low, wg2_ref[...], precision=HIGHEST, preferred_element_type=F32) + bg_ref[...]
    log_a = (jnp.minimum(z, 0.0) - jnp.log1p(jnp.exp(-jnp.abs(z)))) * (1.0 / GLA_TAU)
    hi = log_a.astype(BF16)
    rest = log_a - hi.astype(F32)
    mid = rest.astype(BF16)
    lo = (rest - mid.astype(F32)).astype(BF16)
    pieces = jnp.concatenate([hi, mid, lo], axis=1)

    row_chunk = lax.broadcasted_iota(jnp.int32, (tb, GLA_DK), 0) // c_len
    col_chunk = lax.broadcasted_iota(jnp.int32, (GLA_DK, tb), 1) // c_len
    causal = tri_ref[...] > 0
    contract_last = (((1,), (1,)), ((), ()))
    for blk in range(tc // tb):
        rows = slice(blk * tb, (blk + 1) * tb)
        cum = jnp.dot(tri_ref[...], pieces[rows, :], preferred_element_type=F32)
        b = cum[:, :hk] + cum[:, hk:2 * hk] + cum[:, 2 * hk:]
        b_last = jnp.concatenate(
            [jnp.broadcast_to(b[(c + 1) * c_len - 1:(c + 1) * c_len, :], (c_len, hk)) for c in range(n_chunks)],
            axis=0)
        q_dec = (proj[rows, :hk] * (GLA_DK ** -0.5) * jnp.exp(b)).astype(BF16)
        k_inv = proj[rows, hk:2 * hk] * jnp.exp(-b)
        dec = jnp.exp(b_last)
        k_rem = k_inv * dec
        k_inv = k_inv.astype(BF16)
        for h in range(GLA_HEADS):
            ksl = slice(h * GLA_DK, (h + 1) * GLA_DK)
            vsl = slice(2 * hk + h * GLA_DV, 2 * hk + (h + 1) * GLA_DV)
            rsl = slice(2 * hk + hv + h * GLA_DV, 2 * hk + hv + (h + 1) * GLA_DV)
            qd = q_dec[:, ksl]
            vh = proj[rows, vsl].astype(BF16)
            attn = lax.dot_general(qd, k_inv[:, ksl], contract_last, preferred_element_type=F32)
            o = jnp.dot(jnp.where(causal, attn, 0.0).astype(BF16), vh, preferred_element_type=F32)
            k_t = k_rem[:, ksl].T.astype(BF16)
            kv = jnp.dot(jnp.concatenate([jnp.where(col_chunk == c, k_t, 0.0) for c in range(n_chunks)], axis=0),
                         vh, preferred_element_type=F32)
            state = state_scr[h]
            states = []
            for c in range(n_chunks):
                states.append(state.astype(BF16))
                dec_col = jnp.broadcast_to(dec[c * c_len:c * c_len + 1, ksl], (8, GLA_DK)).T[:, 0:1]
                state = state * dec_col + kv[c * GLA_DK:(c + 1) * GLA_DK, :]
            state_scr[h] = state
            o = o + jnp.dot(jnp.concatenate([jnp.where(row_chunk == c, qd, 0.0) for c in range(n_chunks)], axis=1),
                            jnp.concatenate(states, axis=0), preferred_element_type=F32)
            rr = proj[rows, rsl]
            o_ref[rows, h * GLA_DV:(h + 1) * GLA_DV] = (
                _rms(o, gon_ref[...]) * (rr * jax.nn.sigmoid(rr))).astype(o_ref.dtype)


def _gla(x2, g_in, w, g_on, batch, seq, tc):
    n, d = x2.shape
    hk = GLA_HEADS * GLA_DK
    hv = GLA_HEADS * GLA_DV
    ns = seq // tc
    tb = min(GLA_BLOCK, tc)
    idx = np.arange(tb)
    tri = ((idx[:, None] // GLA_CHUNK == idx[None, :] // GLA_CHUNK) & (idx[None, :] <= idx[:, None]))
    return pl.pallas_call(
        _gla_kernel,
        grid=(batch, ns),
        in_specs=[pl.BlockSpec((tc, d), lambda b, s: (b * ns + s, 0)), _full((1, d)), _full(w["win"].shape),
                  _full(w["wg2"].shape), _full((1, hk)), _full((1, GLA_DV)), _full((tb, tb))],
        out_specs=pl.BlockSpec((tc, hv), lambda b, s: (b * ns + s, 0)),
        out_shape=jax.ShapeDtypeStruct((n, hv), BF16),
        scratch_shapes=[pltpu.VMEM((GLA_HEADS, GLA_DK, GLA_DV), F32)],
        compiler_params=_params(("parallel", "arbitrary")),
        name="gla",
    )(x2, g_in, w["win"], w["wg2"], w["bg"], g_on, jnp.asarray(tri, BF16))


def _prep_mla(w_down, g_q_lat, w_uq, g_kv_lat, w_ukv, g_qn, g_kn):
    d = w_down.shape[0]
    lat = MLA_Q_RANK + MLA_KV_RANK
    z = lambda c: jnp.zeros((d, c), F32)
    wd = jnp.concatenate([w_down[:, :lat], z(MLA_NOPE), w_down[:, lat:], z(LANES - MLA_QK)], axis=1)
    pad_head = LANES - MLA_QK
    wuq = jnp.pad(w_uq.reshape(MLA_Q_RANK, MLA_HEADS, MLA_QK), ((0, 0), (0, 0), (0, pad_head)))
    wukv = w_ukv.reshape(MLA_KV_RANK, MLA_HEADS, MLA_NOPE + MLA_V)
    wuk = jnp.pad(wukv[:, :, :MLA_NOPE], ((0, 0), (0, 0), (0, LANES - MLA_NOPE)))
    wuv = jnp.pad(wukv[:, :, MLA_NOPE:], ((0, 0), (0, 0), (0, LANES - MLA_V)))
    half = MLA_ROPE // 2
    inv_freq = ROPE_THETA ** (-jnp.arange(half, dtype=F32) / half)
    zero = lambda c: jnp.zeros((c,), F32)
    vone = jnp.tile(jnp.concatenate([zero(MLA_V), jnp.ones((LANES - MLA_V,), F32)]), MLA_HEADS)
    freq = jnp.concatenate([zero(MLA_NOPE), inv_freq, inv_freq, zero(pad_head)])
    rotw = np.zeros((2 * LANES, 2 * LANES), np.float32)
    for lane in range(MLA_NOPE, MLA_NOPE + half):
        rotw[lane + half, lane] = -1.0
        rotw[lane, lane + half] = 1.0
    rotw[LANES:, LANES:] = 1.0
    scale = MLA_QK ** -0.5 * math.log2(math.e)
    return {
        "wd": wd.astype(BF16),
        "gql": g_q_lat.reshape(1, -1), "gkl": g_kv_lat.reshape(1, -1),
        "wuq": wuq.reshape(MLA_Q_RANK, MLA_HEADS * LANES).astype(BF16),
        "wuk": wuk.reshape(MLA_KV_RANK, MLA_HEADS * LANES).astype(BF16),
        "wuv": wuv.reshape(MLA_KV_RANK, MLA_HEADS * LANES).astype(BF16),
        "vone": vone.reshape(1, MLA_HEADS * LANES),
        "gqn": (jnp.pad(g_qn, (0, pad_head)) * scale).reshape(1, LANES),
        "gkn": jnp.pad(g_kn, (0, pad_head)).reshape(1, LANES),
        "freq": freq.reshape(1, LANES), "rotw": jnp.asarray(rotw, BF16),
    }


def _prep_gla(w_in, w_g2, b_g):
    pad = LANES - GLA_GATE_RANK
    return {
        "win": jnp.pad(w_in, ((0, 0), (0, pad))).astype(BF16),
        "wg2": jnp.pad(w_g2, ((0, pad), (0, 0))),
        "bg": b_g.reshape(1, -1),
    }


def _prep_peer(w_query, sub_keys):
    zk = jnp.zeros((PEER_NKEYS, PEER_HALF), F32)
    return {
        "wq": _split3_const(w_query, 0),
        "k0": _split3_const(jnp.concatenate([sub_keys[0], zk], axis=1), 1),
        "k1": _split3_const(jnp.concatenate([zk, sub_keys[1]], axis=1), 1),
    }


def _peer_layer(res, a, w_o, g, pw, u_all, v_all, layer, tiles):
    h, xn, idx, a2, th, s1, b2 = _peer_route(res, a, w_o, g.reshape(1, -1), pw["wq"], pw["k0"], pw["k1"],
                                             tiles["route"])
    gate = _peer_weights(idx, a2, th, s1, b2, tiles["weights"])
    gate = gate.reshape(gate.shape[0], -1, PEER_NKEYS)
    return _peer_dense(xn, gate, u_all, v_all, layer, h, tiles["dense_t"], PEER_STEP_KEYS * PEER_NKEYS)


def _tiles(seq):
    pick = lambda want: math.gcd(want, seq)
    return {"proj": pick(512), "attn": pick(2048), "route": pick(512), "weights": pick(128),
            "dense_t": pick(1024), "gla": pick(1024)}


def kernel(x, positions, attn_norm_g, ffn_norm_g, mla_w_down, mla_g_q_lat, mla_w_uq, mla_g_kv_lat,
           mla_w_ukv, mla_g_qn, mla_g_kn, mla_w_o, gla_w_in, gla_w_g2, gla_b_g, gla_g_on, gla_w_o,
           peer_w_query, peer_sub_keys, peer_u, peer_v):
    batch, seq, d = x.shape
    n = batch * seq
    depth = attn_norm_g.shape[0]
    tiles = _tiles(seq)
    h = x.reshape(n, d)
    pos = positions.reshape(n, 1).astype(F32)
    u_all = peer_u.astype(BF16)
    v_all = peer_v.astype(BF16)
    for i in range(depth):
        j = i // 2
        g_in = attn_norm_g[i].reshape(1, d)
        if i % 2 == 0:
            w = _prep_mla(mla_w_down[j], mla_g_q_lat[j], mla_w_uq[j], mla_g_kv_lat[j], mla_w_ukv[j],
                          mla_g_qn[j], mla_g_kn[j])
            q, k, v = _mla_proj(h, pos, g_in, w, tiles["proj"])
            a = _attention(q, k, v, batch, seq, tiles["attn"])
            w_o = mla_w_o[j].astype(BF16)
        else:
            w = _prep_gla(gla_w_in[j], gla_w_g2[j], gla_b_g[j])
            a = _gla(h, g_in, w, gla_g_on[j].reshape(1, -1), batch, seq, tiles["gla"])
            w_o = gla_w_o[j].astype(BF16)
        pw = _prep_peer(peer_w_query[i], peer_sub_keys[i])
        h = _peer_layer(h, a, w_o, ffn_norm_g[i], pw, u_all, v_all, i, tiles)
    return h.reshape(batch, seq, d)
```

```python
import math

import jax
import jax.numpy as jnp
import numpy as np
from jax import lax
from jax.experimental import pallas as pl
from jax.experimental.pallas import tpu as pltpu

F32 = jnp.float32
BF16 = jnp.bfloat16
HIGHEST = lax.Precision.HIGHEST

LANES = 128
SUBLANES = 8
NORM_EPS = 1e-6
VMEM_LIMIT = 56 * 1024 * 1024

MLA_HEADS = 16
MLA_Q_RANK = 384
MLA_KV_RANK = 256
MLA_NOPE = 64
MLA_ROPE = 32
MLA_QK = MLA_NOPE + MLA_ROPE
MLA_V = 64
ROPE_THETA = 10000.0
GLA_HEADS = 4
GLA_DK = 128
GLA_DV = 256
GLA_GATE_RANK = 16
GLA_TAU = 16.0
GLA_CHUNK = 64
GLA_BLOCK = 256
PEER_HEADS = 8
PEER_NKEYS = 128
PEER_HALF = 64
PEER_TOPK = 16
PEER_RANK_ROWS = 24
NEG_INF = float("-inf")
PEER_STEP_KEYS = 8
ATTN_DIAG_ROW_CHUNK = 512
ATTN_ROW_CHUNK = 1024


def _params(sem):
    return pltpu.CompilerParams(dimension_semantics=sem, vmem_limit_bytes=VMEM_LIMIT)


def _rms(x, g, n=None):
    n = x.shape[-1] if n is None else n
    ss = jnp.sum(x * x, axis=-1, keepdims=True)
    return x * lax.rsqrt(ss * (1.0 / n) + NORM_EPS) * g


def _split3(x):
    hi = x.astype(BF16)
    lo = (x - hi.astype(F32)).astype(BF16)
    return jnp.concatenate([hi, lo, hi], axis=-1)


def _split3_const(w, axis):
    bits = lax.bitcast_convert_type(w, jnp.uint32) & jnp.uint32(0xFFFF0000)
    hi = lax.bitcast_convert_type(bits, F32)
    lo = (w - hi).astype(BF16)
    hi = hi.astype(BF16)
    return jnp.concatenate([hi, hi, lo], axis=axis)


def _full(shape):
    nd = len(shape)
    return pl.BlockSpec(shape, lambda *_: (0,) * nd)


def _mla_proj_kernel(x_ref, pos_ref, gin_ref, wd_ref, gql_ref, wuq_ref, gkl_ref, wuk_ref, wuv_ref,
                     gqn_ref, gkn_ref, freq_ref, rotw_ref, vone_ref, q_ref, k_ref, v_ref):
    x = x_ref[...]
    hn = _rms(x, gin_ref[...])
    down = jnp.dot(hn.astype(BF16), wd_ref[...], preferred_element_type=F32)
    c_q = _rms(down[:, :MLA_Q_RANK], gql_ref[...]).astype(BF16)
    c_kv = _rms(down[:, MLA_Q_RANK:MLA_Q_RANK + MLA_KV_RANK], gkl_ref[...]).astype(BF16)
    kr = down[:, MLA_Q_RANK + MLA_KV_RANK:]
    qf = jnp.dot(c_q, wuq_ref[...], preferred_element_type=F32)
    kf = jnp.dot(c_kv, wuk_ref[...], preferred_element_type=F32)
    v_ref[...] = (jnp.dot(c_kv, wuv_ref[...], preferred_element_type=F32) + vone_ref[...]).astype(v_ref.dtype)

    ang = pos_ref[...] * freq_ref[...]
    cos = jnp.cos(ang)
    sin = jnp.sin(ang)

    def head(t, g):
        tg = t * g
        res = jnp.dot(jnp.concatenate([tg.astype(BF16), (t * t).astype(BF16)], axis=1), rotw_ref[...],
                      preferred_element_type=F32)
        r = lax.rsqrt(res[:, LANES:] * (1.0 / MLA_QK) + NORM_EPS)
        return r * (tg * cos + res[:, :LANES] * sin)

    for h in range(MLA_HEADS):
        sl = slice(h * LANES, (h + 1) * LANES)
        q_ref[:, sl] = head(qf[:, sl], gqn_ref[...]).astype(q_ref.dtype)
        k_ref[:, sl] = head(kf[:, sl] + kr, gkn_ref[...]).astype(k_ref.dtype)


def _mla_proj(x2, pos2, g_in, w, tm):
    n, d = x2.shape
    hq = MLA_HEADS * LANES
    row = lambda c: pl.BlockSpec((tm, c), lambda i: (i, 0))
    return pl.pallas_call(
        _mla_proj_kernel,
        grid=(n // tm,),
        in_specs=[row(d), row(1), _full((1, d)), _full(w["wd"].shape), _full((1, MLA_Q_RANK)),
                  _full(w["wuq"].shape), _full((1, MLA_KV_RANK)), _full(w["wuk"].shape),
                  _full(w["wuv"].shape), _full((1, LANES)), _full((1, LANES)), _full((1, LANES)),
                  _full(w["rotw"].shape), _full((1, hq))],
        out_specs=[row(hq), row(hq), row(hq)],
        out_shape=[jax.ShapeDtypeStruct((n, hq), BF16)] * 3,
        compiler_params=_params(("parallel",)),
        name="mla_proj",
    )(x2, pos2, g_in, w["wd"], w["gql"], w["wuq"], w["gkl"], w["wuk"], w["wuv"], w["gqn"], w["gkn"],
      w["freq"], w["rotw"], w["vone"])


def _attn_kernel(qi_ref, ki_ref, q_ref, k_ref, v_ref, o_ref, m_scr, acc_scr):
    p_id = pl.program_id(2)
    qi = qi_ref[p_id]
    ki = ki_ref[p_id]
    tq = q_ref.shape[0]
    tk = k_ref.shape[0]

    @pl.when(ki == 0)
    def _():
        m_scr[...] = jnp.full(m_scr.shape, NEG_INF, F32)
        acc_scr[...] = jnp.zeros(acc_scr.shape, F32)

    def step(diagonal):
        rq = min(ATTN_DIAG_ROW_CHUNK if diagonal else ATTN_ROW_CHUNK, tq)
        for hh in range(2):
            sl = slice(hh * LANES, (hh + 1) * LANES)
            for r in range(tq // rq):
                rows = slice(r * rq, (r + 1) * rq)
                nk = (r + 1) * rq if diagonal else tk
                s = lax.dot_general(q_ref[rows, sl], k_ref[:nk, sl], (((1,), (1,)), ((), ())),
                                    preferred_element_type=F32)
                if diagonal:
                    row = lax.broadcasted_iota(jnp.int32, (rq, rq), 0)
                    col = lax.broadcasted_iota(jnp.int32, (rq, rq), 1)
                    tail = jnp.where(col <= row, s[:, r * rq:], NEG_INF)
                    s = tail if r == 0 else jnp.concatenate([s[:, :r * rq], tail], axis=1)
                m_prev = m_scr[hh, rows]
                m_new = jnp.maximum(m_prev, jnp.max(s, axis=-1, keepdims=True))
                alpha = jnp.exp2(m_prev - m_new)
                p = jnp.concatenate(
                    [jnp.exp2(s[:, c * LANES:(c + 1) * LANES] - m_new).astype(BF16)
                     for c in range(nk // LANES)], axis=1)
                pv = jnp.dot(p, v_ref[:nk, sl], preferred_element_type=F32)
                acc_scr[hh, rows] = alpha * acc_scr[hh, rows] + pv
                m_scr[hh, rows] = m_new

    @pl.when(ki < qi)
    def _():
        step(False)

    @pl.when(ki == qi)
    def _():
        step(True)
        lane = lax.broadcasted_iota(jnp.int32, (tq, LANES), 1)
        a0 = acc_scr[0]
        a1 = acc_scr[1]
        o0 = a0 / pltpu.roll(a0, MLA_V, 1)
        o1 = pltpu.roll(a1, MLA_V, 1) / a1
        o_ref[...] = jnp.where(lane < MLA_V, o0, o1).astype(o_ref.dtype)


def _attention(q, k, v, batch, seq, tq):
    nq = seq // tq
    qi_tab = np.array([qi for qi in range(nq) for _ in range(qi + 1)], np.int32)
    ki_tab = np.array([ki for qi in range(nq) for ki in range(qi + 1)], np.int32)
    npairs = len(qi_tab)
    hp = MLA_HEADS // 2
    blk = lambda tab: pl.BlockSpec((tq, 2 * LANES), lambda b, h, p, qt, kt: (b * nq + (qt, kt)[tab][p], h))
    grid_spec = pltpu.PrefetchScalarGridSpec(
        num_scalar_prefetch=2,
        grid=(batch, hp, npairs),
        in_specs=[blk(0), blk(1), blk(1)],
        out_specs=pl.BlockSpec((tq, 2 * MLA_V), lambda b, h, p, qt, kt: (b * nq + qt[p], h)),
        scratch_shapes=[pltpu.VMEM((2, tq, LANES), F32), pltpu.VMEM((2, tq, LANES), F32)],
    )
    return pl.pallas_call(
        _attn_kernel,
        grid_spec=grid_spec,
        out_shape=jax.ShapeDtypeStruct((batch * seq, MLA_HEADS * MLA_V), BF16),
        compiler_params=_params(("parallel", "parallel", "arbitrary")),
        name="mla_attention",
    )(jnp.asarray(qi_tab), jnp.asarray(ki_tab), q, k, v)


def _sort_network(n):
    pairs = []
    p = 1
    while p < n:
        k = p
        while k >= 1:
            for j in range(k % p, n - k, 2 * k):
                for i in range(min(k, n - j - k)):
                    if (i + j) // (2 * p) == (i + j + k) // (2 * p):
                        pairs.append((i + j, i + j + k))
            k //= 2
        p *= 2
    return pairs


def _sorted_top(cur, n, val_scr, idx_scr=None):
    sub = SUBLANES
    groups = cur.shape[0] // sub
    track = idx_scr is not None
    base = lax.broadcasted_iota(jnp.int32, (sub, LANES), 0).astype(F32)
    network = _sort_network(groups)
    for c in range(cur.shape[1] // LANES):
        csl = slice(c * LANES, (c + 1) * LANES)
        vals = [cur[g * sub:(g + 1) * sub, csl] for g in range(groups)]
        if track:
            idxs = [base + float(g * sub) for g in range(groups)]
        for a, b in network:
            if track:
                keep = vals[a] >= vals[b]
                idxs[a], idxs[b] = jnp.where(keep, idxs[a], idxs[b]), jnp.where(keep, idxs[b], idxs[a])
            vals[a], vals[b] = jnp.maximum(vals[a], vals[b]), jnp.minimum(vals[a], vals[b])
        for r in range(n):
            m = jnp.max(vals[0], axis=0, keepdims=True)
            val_scr[r:r + 1, csl] = m
            hit = vals[0] == m
            if track:
                first = jnp.min(jnp.where(hit, base, float(sub)), axis=0, keepdims=True)
                hit = base == first
                idx_scr[r:r + 1, csl] = jnp.max(jnp.where(hit, idxs[0], -1.0), axis=0, keepdims=True)
            depth = min(n - 1 - r, groups)
            for k in range(depth):
                below = vals[k + 1] if k + 1 < groups else NEG_INF
                vals[k] = jnp.where(hit, below, vals[k])
                if track and k + 1 < groups:
                    idxs[k] = jnp.where(hit, idxs[k + 1], idxs[k])


def _peer_route_kernel(res_ref, a_ref, wo_ref, g_ref, wq_ref, k0_ref, k1_ref,
                       h_ref, x_ref, idx_ref, a2_ref, th_ref, s1_ref, b2_ref,
                       a_scr, b_scr, c_scr, i_scr, idx_all, a2_all):
    tr = res_ref.shape[0]
    h = res_ref[...] + jnp.dot(a_ref[...], wo_ref[...], preferred_element_type=F32)
    h_ref[...] = h
    hn = _rms(h, g_ref[...])
    x_ref[...] = (hn * (1.0 / math.sqrt(2.0))).astype(x_ref.dtype)
    q = jnp.dot(_split3(hn), wq_ref[...], preferred_element_type=F32)
    q3 = [_split3(q[:, hd * LANES:(hd + 1) * LANES]) for hd in range(PEER_HEADS)]
    nk = PEER_TOPK + 1
    rows = lax.broadcasted_iota(jnp.int32, (SUBLANES, tr), 0)
    a_scr[...] = jnp.full(a_scr.shape, NEG_INF, F32)
    b_scr[...] = jnp.full(b_scr.shape, NEG_INF, F32)
    contract_last = (((1,), (1,)), ((), ()))
    log2e = math.log2(math.e)
    for hd in range(PEER_HEADS):
        s0 = lax.dot_general(k0_ref[...], q3[hd], contract_last, preferred_element_type=F32)
        s1 = lax.dot_general(k1_ref[...], q3[hd], contract_last, preferred_element_type=F32)
        _sorted_top(s0, nk, a_scr, i_scr)
        _sorted_top(s1, nk, b_scr)
        pieces = [b_scr[...] + a_scr[0:1, :]]
        r = 1
        while nk // (r + 1) >= 2:
            pieces.append(jnp.where(rows < nk // (r + 1), b_scr[0:SUBLANES, :] + a_scr[r:r + 1, :], NEG_INF))
            r += 1
        assert r == SUBLANES
        pieces.append(a_scr[SUBLANES:PEER_RANK_ROWS, :] + b_scr[0:1, :])
        cand = jnp.concatenate(pieces, axis=0)
        padded = jnp.concatenate([cand, jnp.full((PEER_NKEYS - cand.shape[0], tr), NEG_INF, F32)], axis=0)
        _sorted_top(padded, nk, c_scr)
        best = c_scr[0:1, :]
        tau = 0.5 * (c_scr[PEER_TOPK - 1:PEER_TOPK, :] + c_scr[PEER_TOPK:PEER_TOPK + 1, :])
        z = jnp.sum(jnp.where(cand >= tau, jnp.exp(cand - best), 0.0), axis=0, keepdims=True)
        a_max = a_scr[0:1, :]
        a_rel = (a_scr[0:PEER_TOPK, :] - a_max) * log2e
        s1_rel = (s1 - (best - a_max + jnp.log(z) + 0.5 * math.log(2.0))) * log2e
        thr = (tau - (best + jnp.log(z) + 0.5 * math.log(2.0))) * log2e
        slots = slice(hd * PEER_TOPK, (hd + 1) * PEER_TOPK)
        idx_all[slots, :] = i_scr[0:PEER_TOPK, :]
        a2_all[slots, :] = jnp.exp2(a_rel)
        th_ref[slots, :] = thr - a_rel
        s1t = s1_rel.T
        s1_ref[hd] = s1t
        b2_ref[hd] = jnp.exp2(s1t)
    idx_ref[...] = idx_all[...].T
    a2_ref[...] = a2_all[...].T


def _peer_route(res, a, w_o, g, wq, k0, k1, tr):
    n, d = res.shape
    da = a.shape[1]
    nh = PEER_HEADS
    ns = nh * PEER_TOPK
    row = lambda c: pl.BlockSpec((tr, c), lambda i: (i, 0))
    per_head = pl.BlockSpec((nh, tr, PEER_NKEYS), lambda i: (0, i, 0))
    return pl.pallas_call(
        _peer_route_kernel,
        grid=(n // tr,),
        in_specs=[row(d), row(da), _full(w_o.shape), _full((1, d)), _full(wq.shape), _full(k0.shape),
                  _full(k1.shape)],
        out_specs=[row(d), row(d), row(ns), row(ns), pl.BlockSpec((ns, tr), lambda i: (0, i)),
                   per_head, per_head],
        out_shape=[jax.ShapeDtypeStruct((n, d), F32), jax.ShapeDtypeStruct((n, d), BF16),
                   jax.ShapeDtypeStruct((n, ns), F32), jax.ShapeDtypeStruct((n, ns), F32),
                   jax.ShapeDtypeStruct((ns, n), F32),
                   jax.ShapeDtypeStruct((nh, n, PEER_NKEYS), F32),
                   jax.ShapeDtypeStruct((nh, n, PEER_NKEYS), F32)],
        scratch_shapes=[pltpu.VMEM((PEER_RANK_ROWS, tr), F32)] * 4 + [pltpu.VMEM((ns, tr), F32)] * 2,
        compiler_params=_params(("parallel",)),
        name="peer_route",
    )(res, a, w_o, g, wq, k0, k1)


def _peer_weights_kernel(idx_ref, a2_ref, th_ref, s1_ref, b2_ref, g_ref):
    tg = idx_ref.shape[0]
    ns = th_ref.shape[0]
    sub = lax.broadcasted_iota(jnp.int32, (PEER_NKEYS, ns), 0).astype(F32)

    def head_rows(ref, t):
        return jnp.concatenate([jnp.broadcast_to(ref[hd, t:t + 1, :], (PEER_TOPK, PEER_NKEYS))
                                for hd in range(PEER_HEADS)], axis=0)

    for t in range(tg):
        first = jnp.where(sub == idx_ref[t:t + 1, :], a2_ref[t:t + 1, :], 0.0).astype(BF16)
        thr = jnp.broadcast_to(th_ref[:, t:t + 1], (ns, PEER_NKEYS))
        second = jnp.where(head_rows(s1_ref, t) >= thr, head_rows(b2_ref, t), 0.0).astype(BF16)
        gt = jnp.dot(first, second, preferred_element_type=F32)
        for blk in range(PEER_NKEYS // PEER_STEP_KEYS):
            g_ref[blk, t] = gt[blk * PEER_STEP_KEYS:(blk + 1) * PEER_STEP_KEYS, :]


def _peer_weights(idx, a2, th, s1, b2, tg):
    n, ns = idx.shape
    nh = PEER_HEADS
    nb = PEER_NKEYS // PEER_STEP_KEYS
    row = pl.BlockSpec((tg, ns), lambda i: (i, 0))
    per_head = pl.BlockSpec((nh, tg, PEER_NKEYS), lambda i: (0, i, 0))
    return pl.pallas_call(
        _peer_weights_kernel,
        grid=(n // tg,),
        in_specs=[row, row, pl.BlockSpec((ns, tg), lambda i: (0, i)), per_head, per_head],
        out_specs=pl.BlockSpec((nb, tg, PEER_STEP_KEYS, PEER_NKEYS), lambda i: (0, i, 0, 0)),
        out_shape=jax.ShapeDtypeStruct((nb, n, PEER_STEP_KEYS, PEER_NKEYS), F32),
        compiler_params=_params(("parallel",)),
        name="peer_weights",
    )(idx, a2, th, s1, b2)


def _peer_dense_kernel(x_ref, g_ref, u_ref, v_ref, h_ref, o_ref, acc_scr):
    e = pl.program_id(1)
    ne = pl.num_programs(1)
    t = x_ref.shape[0]
    rows_per_step = PEER_STEP_KEYS

    @pl.when(e == 0)
    def _():
        acc_scr[...] = jnp.zeros(acc_scr.shape, F32)

    y = lax.dot_general(x_ref[...], u_ref[...], (((1,), (1,)), ((), ())),
                        preferred_element_type=F32)
    ps = []
    for ii in range(rows_per_step):
        yc = y[:, ii * PEER_NKEYS:(ii + 1) * PEER_NKEYS]
        gi = g_ref[pl.ds(ii, t, stride=rows_per_step), :]
        ps.append((gi * (yc * (1.0 + lax.erf(yc)))).astype(BF16))
    acc_scr[...] += jnp.dot(jnp.concatenate(ps, axis=1), v_ref[...], preferred_element_type=F32)

    @pl.when(e == ne - 1)
    def _():
        o_ref[...] = h_ref[...] + acc_scr[...]


def _peer_dense(x, g, u, v, layer, h, t, eb):
    n, d = x.shape
    ne = u.shape[1] // eb
    tok = lambda c: pl.BlockSpec((t, c), lambda i, e: (i, 0))
    return pl.pallas_call(
        _peer_dense_kernel,
        grid=(n // t, ne),
        in_specs=[tok(d),
                  pl.BlockSpec((None, t * PEER_STEP_KEYS, PEER_NKEYS), lambda i, e: (e, i, 0)),
                  pl.BlockSpec((None, eb, d), lambda i, e: (layer, e, 0)),
                  pl.BlockSpec((None, eb, d), lambda i, e: (layer, e, 0)),
                  pl.BlockSpec((t, d), lambda i, e: (i, 0), pipeline_mode=pl.Buffered(1))],
        out_specs=tok(d),
        out_shape=jax.ShapeDtypeStruct((n, d), F32),
        scratch_shapes=[pltpu.VMEM((t, d), F32)],
        compiler_params=_params(("parallel", "arbitrary")),
        name="peer_dense",
    )(x, g, u, v, h)


def _gla_kernel(x_ref, gin_ref, win_ref, wg2_ref, bg_ref, gon_ref, tri_ref, o_ref, state_scr):
    tc = x_ref.shape[0]
    tb = tri_ref.shape[0]
    c_len = GLA_CHUNK
    n_chunks = tb // c_len
    hk = GLA_HEADS * GLA_DK
    hv = GLA_HEADS * GLA_DV

    @pl.when(pl.program_id(1) == 0)
    def _():
        state_scr[...] = jnp.zeros(state_scr.shape, F32)

    hn = _rms(x_ref[...], gin_ref[...])
    proj = jnp.dot(hn.astype(BF16), win_ref[...], preferred_element_type=F32)
    g_low = proj[:, 2 * hk + 2 * hv:]
    z = jnp.dot(g_low, wg2_ref[...], precision=HIGHEST, preferred_element_type=F32) + bg_ref[...]
    log_a = (jnp.minimum(z, 0.0) - jnp.log1p(jnp.exp(-jnp.abs(z)))) * (1.0 / GLA_TAU)
    hi = log_a.astype(BF16)
    rest = log_a - hi.astype(F32)
    mid = rest.astype(BF16)
    lo = (rest - mid.astype(F32)).astype(BF16)
    pieces = jnp.concatenate([hi, mid, lo], axis=1)

    row_chunk = lax.broadcasted_iota(jnp.int32, (tb, GLA_DK), 0) // c_len
    col_chunk = lax.broadcasted_iota(jnp.int32, (GLA_DK, tb), 1) // c_len
    causal = tri_ref[...] > 0
    contract_last = (((1,), (1,)), ((), ()))
    for blk in range(tc // tb):
        rows = slice(blk * tb, (blk + 1) * tb)
        cum = jnp.dot(tri_ref[...], pieces[rows, :], preferred_element_type=F32)
        b = cum[:, :hk] + cum[:, hk:2 * hk] + cum[:, 2 * hk:]
        b_last = jnp.concatenate(
            [jnp.broadcast_to(b[(c + 1) * c_len - 1:(c + 1) * c_len, :], (c_len, hk)) for c in range(n_chunks)],
            axis=0)
        q_dec = (proj[rows, :hk] * (GLA_DK ** -0.5) * jnp.exp(b)).astype(BF16)
        k_inv = proj[rows, hk:2 * hk] * jnp.exp(-b)
        dec = jnp.exp(b_last)
        k_rem = k_inv * dec
        k_inv = k_inv.astype(BF16)
        for h in range(GLA_HEADS):
            ksl = slice(h * GLA_DK, (h + 1) * GLA_DK)
            vsl = slice(2 * hk + h * GLA_DV, 2 * hk + (h + 1) * GLA_DV)
            rsl = slice(2 * hk + hv + h * GLA_DV, 2 * hk + hv + (h + 1) * GLA_DV)
            qd = q_dec[:, ksl]
            vh = proj[rows, vsl].astype(BF16)
            attn = lax.dot_general(qd, k_inv[:, ksl], contract_last, preferred_element_type=F32)
            o = jnp.dot(jnp.where(causal, attn, 0.0).astype(BF16), vh, preferred_element_type=F32)
            k_t = k_rem[:, ksl].T.astype(BF16)
            kv = jnp.dot(jnp.concatenate([jnp.where(col_chunk == c, k_t, 0.0) for c in range(n_chunks)], axis=0),
                         vh, preferred_element_type=F32)
            state = state_scr[h]
            states = []
            for c in range(n_chunks):
                states.append(state.astype(BF16))
                dec_col = jnp.broadcast_to(dec[c * c_len:c * c_len + 1, ksl], (8, GLA_DK)).T[:, 0:1]
                state = state * dec_col + kv[c * GLA_DK:(c + 1) * GLA_DK, :]
            state_scr[h] = state
            o = o + jnp.dot(jnp.concatenate([jnp.where(row_chunk == c, qd, 0.0) for c in range(n_chunks)], axis=1),
                            jnp.concatenate(states, axis=0), preferred_element_type=F32)
            rr = proj[rows, rsl]
            o_ref[rows, h * GLA_DV:(h + 1) * GLA_DV] = (
                _rms(o, gon_ref[...]) * (rr * jax.nn.sigmoid(rr))).astype(o_ref.dtype)


def _gla(x2, g_in, w, g_on, batch, seq, tc):
    n, d = x2.shape
    hk = GLA_HEADS * GLA_DK
    hv = GLA_HEADS * GLA_DV
    ns = seq // tc
    tb = min(GLA_BLOCK, tc)
    idx = np.arange(tb)
    tri = ((idx[:, None] // GLA_CHUNK == idx[None, :] // GLA_CHUNK) & (idx[None, :] <= idx[:, None]))
    return pl.pallas_call(
        _gla_kernel,
        grid=(batch, ns),
        in_specs=[pl.BlockSpec((tc, d), lambda b, s: (b * ns + s, 0)), _full((1, d)), _full(w["win"].shape),
                  _full(w["wg2"].shape), _full((1, hk)), _full((1, GLA_DV)), _full((tb, tb))],
        out_specs=pl.BlockSpec((tc, hv), lambda b, s: (b * ns + s, 0)),
        out_shape=jax.ShapeDtypeStruct((n, hv), BF16),
        scratch_shapes=[pltpu.VMEM((GLA_HEADS, GLA_DK, GLA_DV), F32)],
        compiler_params=_params(("parallel", "arbitrary")),
        name="gla",
    )(x2, g_in, w["win"], w["wg2"], w["bg"], g_on, jnp.asarray(tri, BF16))


def _prep_mla(w_down, g_q_lat, w_uq, g_kv_lat, w_ukv, g_qn, g_kn):
    d = w_down.shape[0]
    lat = MLA_Q_RANK + MLA_KV_RANK
    z = lambda c: jnp.zeros((d, c), F32)
    wd = jnp.concatenate([w_down[:, :lat], z(MLA_NOPE), w_down[:, lat:], z(LANES - MLA_QK)], axis=1)
    pad_head = LANES - MLA_QK
    wuq = jnp.pad(w_uq.reshape(MLA_Q_RANK, MLA_HEADS, MLA_QK), ((0, 0), (0, 0), (0, pad_head)))
    wukv = w_ukv.reshape(MLA_KV_RANK, MLA_HEADS, MLA_NOPE + MLA_V)
    wuk = jnp.pad(wukv[:, :, :MLA_NOPE], ((0, 0), (0, 0), (0, LANES - MLA_NOPE)))
    wuv = jnp.pad(wukv[:, :, MLA_NOPE:], ((0, 0), (0, 0), (0, LANES - MLA_V)))
    half = MLA_ROPE // 2
    inv_freq = ROPE_THETA ** (-jnp.arange(half, dtype=F32) / half)
    zero = lambda c: jnp.zeros((c,), F32)
    vone = jnp.tile(jnp.concatenate([zero(MLA_V), jnp.ones((LANES - MLA_V,), F32)]), MLA_HEADS)
    freq = jnp.concatenate([zero(MLA_NOPE), inv_freq, inv_freq, zero(pad_head)])
    rotw = np.zeros((2 * LANES, 2 * LANES), np.float32)
    for lane in range(MLA_NOPE, MLA_NOPE + half):
        rotw[lane + half, lane] = -1.0
        rotw[lane, lane + half] = 1.0
    rotw[LANES:, LANES:] = 1.0
    scale = MLA_QK ** -0.5 * math.log2(math.e)
    return {
        "wd": wd.astype(BF16),
        "gql": g_q_lat.reshape(1, -1), "gkl": g_kv_lat.reshape(1, -1),
        "wuq": wuq.reshape(MLA_Q_RANK, MLA_HEADS * LANES).astype(BF16),
        "wuk": wuk.reshape(MLA_KV_RANK, MLA_HEADS * LANES).astype(BF16),
        "wuv": wuv.reshape(MLA_KV_RANK, MLA_HEADS * LANES).astype(BF16),
        "vone": vone.reshape(1, MLA_HEADS * LANES),
        "gqn": (jnp.pad(g_qn, (0, pad_head)) * scale).reshape(1, LANES),
        "gkn": jnp.pad(g_kn, (0, pad_head)).reshape(1, LANES),
        "freq": freq.reshape(1, LANES), "rotw": jnp.asarray(rotw, BF16),
    }


def _prep_gla(w_in, w_g2, b_g):
    pad = LANES - GLA_GATE_RANK
    return {
        "win": jnp.pad(w_in, ((0, 0), (0, pad))).astype(BF16),
        "wg2": jnp.pad(w_g2, ((0, pad), (0, 0))),
        "bg": b_g.reshape(1, -1),
    }


def _prep_peer(w_query, sub_keys):
    zk = jnp.zeros((PEER_NKEYS, PEER_HALF), F32)
    return {
        "wq": _split3_const(w_query, 0),
        "k0": _split3_const(jnp.concatenate([sub_keys[0], zk], axis=1), 1),
        "k1": _split3_const(jnp.concatenate([zk, sub_keys[1]], axis=1), 1),
    }


def _peer_layer(res, a, w_o, g, pw, u_all, v_all, layer, tiles):
    h, xn, idx, a2, th, s1, b2 = _peer_route(res, a, w_o, g.reshape(1, -1), pw["wq"], pw["k0"], pw["k1"],
                                             tiles["route"])
    gate = _peer_weights(idx, a2, th, s1, b2, tiles["weights"])
    gate = gate.reshape(gate.shape[0], -1, PEER_NKEYS)
    return _peer_dense(xn, gate, u_all, v_all, layer, h, tiles["dense_t"], PEER_STEP_KEYS * PEER_NKEYS)


def _tiles(seq):
    pick = lambda want: math.gcd(want, seq)
    return {"proj": pick(512), "attn": pick(2048), "route": pick(512), "weights": pick(256),
            "dense_t": pick(1024), "gla": pick(1024)}


def kernel(x, positions, attn_norm_g, ffn_norm_g, mla_w_down, mla_g_q_lat, mla_w_uq, mla_g_kv_lat,
           mla_w_ukv, mla_g_qn, mla_g_kn, mla_w_o, gla_w_in, gla_w_g2, gla_b_g, gla_g_on, gla_w_o,
           peer_w_query, peer_sub_keys, peer_u, peer_v):
    batch, seq, d = x.shape
    n = batch * seq
    depth = attn_norm_g.shape[0]
    tiles = _tiles(seq)
    h = x.reshape(n, d)
    pos = positions.reshape(n, 1).astype(F32)
    u_all = peer_u.astype(BF16)
    v_all = peer_v.astype(BF16)
    for i in range(depth):
        j = i // 2
        g_in = attn_norm_g[i].reshape(1, d)
        if i % 2 == 0:
            w = _prep_mla(mla_w_down[j], mla_g_q_lat[j], mla_w_uq[j], mla_g_kv_lat[j], mla_w_ukv[j],
                          mla_g_qn[j], mla_g_kn[j])
            q, k, v = _mla_proj(h, pos, g_in, w, tiles["proj"])
            a = _attention(q, k, v, batch, seq, tiles["attn"])
            w_o = mla_w_o[j].astype(BF16)
        else:
            w = _prep_gla(gla_w_in[j], gla_w_g2[j], gla_b_g[j])
            a = _gla(h, g_in, w, gla_g_on[j].reshape(1, -1), batch, seq, tiles["gla"])
            w_o = gla_w_o[j].astype(BF16)
        pw = _prep_peer(peer_w_query[i], peer_sub_keys[i])
        h = _peer_layer(h, a, w_o, ffn_norm_g[i], pw, u_all, v_all, i, tiles)
    return h.reshape(batch, seq, d)
```

```python
import math

import jax
import jax.numpy as jnp
import numpy as np
from jax import lax
from jax.experimental import pallas as pl
from jax.experimental.pallas import tpu as pltpu

F32 = jnp.float32
BF16 = jnp.bfloat16
HIGHEST = lax.Precision.HIGHEST

LANES = 128
SUBLANES = 8
NORM_EPS = 1e-6
VMEM_LIMIT = 56 * 1024 * 1024

MLA_HEADS = 16
MLA_Q_RANK = 384
MLA_KV_RANK = 256
MLA_NOPE = 64
MLA_ROPE = 32
MLA_QK = MLA_NOPE + MLA_ROPE
MLA_V = 64
ROPE_THETA = 10000.0
GLA_HEADS = 4
GLA_DK = 128
GLA_DV = 256
GLA_GATE_RANK = 16
GLA_TAU = 16.0
GLA_CHUNK = 64
GLA_BLOCK = 256
PEER_HEADS = 8
PEER_NKEYS = 128
PEER_HALF = 64
PEER_TOPK = 16
PEER_RANK_ROWS = 24
NEG_INF = float("-inf")
PEER_STEP_KEYS = 8
ATTN_DIAG_ROW_CHUNK = 512
ATTN_ROW_CHUNK = 1024


def _params(sem):
    return pltpu.CompilerParams(dimension_semantics=sem, vmem_limit_bytes=VMEM_LIMIT)


def _rms(x, g, n=None):
    n = x.shape[-1] if n is None else n
    ss = jnp.sum(x * x, axis=-1, keepdims=True)
    return x * lax.rsqrt(ss * (1.0 / n) + NORM_EPS) * g


def _split3(x):
    hi = x.astype(BF16)
    lo = (x - hi.astype(F32)).astype(BF16)
    return jnp.concatenate([hi, lo, hi], axis=-1)


def _split3_const(w, axis):
    bits = lax.bitcast_convert_type(w, jnp.uint32) & jnp.uint32(0xFFFF0000)
    hi = lax.bitcast_convert_type(bits, F32)
    lo = (w - hi).astype(BF16)
    hi = hi.astype(BF16)
    return jnp.concatenate([hi, hi, lo], axis=axis)


def _full(shape):
    nd = len(shape)
    return pl.BlockSpec(shape, lambda *_: (0,) * nd)


def _mla_proj_kernel(x_ref, pos_ref, gin_ref, wd_ref, gql_ref, wuq_ref, gkl_ref, wuk_ref, wuv_ref,
                     gqn_ref, gkn_ref, freq_ref, rotw_ref, vone_ref, q_ref, k_ref, v_ref):
    x = x_ref[...]
    hn = _rms(x, gin_ref[...])
    down = jnp.dot(hn.astype(BF16), wd_ref[...], preferred_element_type=F32)
    c_q = _rms(down[:, :MLA_Q_RANK], gql_ref[...]).astype(BF16)
    c_kv = _rms(down[:, MLA_Q_RANK:MLA_Q_RANK + MLA_KV_RANK], gkl_ref[...]).astype(BF16)
    kr = down[:, MLA_Q_RANK + MLA_KV_RANK:]
    qf = jnp.dot(c_q, wuq_ref[...], preferred_element_type=F32)
    kf = jnp.dot(c_kv, wuk_ref[...], preferred_element_type=F32)
    v_ref[...] = (jnp.dot(c_kv, wuv_ref[...], preferred_element_type=F32) + vone_ref[...]).astype(v_ref.dtype)

    ang = pos_ref[...] * freq_ref[...]
    cos = jnp.cos(ang)
    sin = jnp.sin(ang)

    def head(t, g):
        tg = t * g
        res = jnp.dot(jnp.concatenate([tg.astype(BF16), (t * t).astype(BF16)], axis=1), rotw_ref[...],
                      preferred_element_type=F32)
        r = lax.rsqrt(res[:, LANES:] * (1.0 / MLA_QK) + NORM_EPS)
        return r * (tg * cos + res[:, :LANES] * sin)

    for h in range(MLA_HEADS):
        sl = slice(h * LANES, (h + 1) * LANES)
        q_ref[:, sl] = head(qf[:, sl], gqn_ref[...]).astype(q_ref.dtype)
        k_ref[:, sl] = head(kf[:, sl] + kr, gkn_ref[...]).astype(k_ref.dtype)


def _mla_proj(x2, pos2, g_in, w, tm):
    n, d = x2.shape
    hq = MLA_HEADS * LANES
    row = lambda c: pl.BlockSpec((tm, c), lambda i: (i, 0))
    return pl.pallas_call(
        _mla_proj_kernel,
        grid=(n // tm,),
        in_specs=[row(d), row(1), _full((1, d)), _full(w["wd"].shape), _full((1, MLA_Q_RANK)),
                  _full(w["wuq"].shape), _full((1, MLA_KV_RANK)), _full(w["wuk"].shape),
                  _full(w["wuv"].shape), _full((1, LANES)), _full((1, LANES)), _full((1, LANES)),
                  _full(w["rotw"].shape), _full((1, hq))],
        out_specs=[row(hq), row(hq), row(hq)],
        out_shape=[jax.ShapeDtypeStruct((n, hq), BF16)] * 3,
        compiler_params=_params(("parallel",)),
        name="mla_proj",
    )(x2, pos2, g_in, w["wd"], w["gql"], w["wuq"], w["gkl"], w["wuk"], w["wuv"], w["gqn"], w["gkn"],
      w["freq"], w["rotw"], w["vone"])


def _attn_kernel(qi_ref, ki_ref, q_ref, k_ref, v_ref, o_ref, m_scr, acc_scr):
    p_id = pl.program_id(2)
    qi = qi_ref[p_id]
    ki = ki_ref[p_id]
    tq = q_ref.shape[0]
    tk = k_ref.shape[0]

    @pl.when(ki == 0)
    def _():
        m_scr[...] = jnp.full(m_scr.shape, NEG_INF, F32)
        acc_scr[...] = jnp.zeros(acc_scr.shape, F32)

    def step(diagonal):
        rq = min(ATTN_DIAG_ROW_CHUNK if diagonal else ATTN_ROW_CHUNK, tq)
        for hh in range(2):
            sl = slice(hh * LANES, (hh + 1) * LANES)
            for r in range(tq // rq):
                rows = slice(r * rq, (r + 1) * rq)
                nk = (r + 1) * rq if diagonal else tk
                s = lax.dot_general(q_ref[rows, sl], k_ref[:nk, sl], (((1,), (1,)), ((), ())),
                                    preferred_element_type=F32)
                if diagonal:
                    row = lax.broadcasted_iota(jnp.int32, (rq, rq), 0)
                    col = lax.broadcasted_iota(jnp.int32, (rq, rq), 1)
                    tail = jnp.where(col <= row, s[:, r * rq:], NEG_INF)
                    s = tail if r == 0 else jnp.concatenate([s[:, :r * rq], tail], axis=1)
                m_prev = m_scr[hh, rows]
                m_new = jnp.maximum(m_prev, jnp.max(s, axis=-1, keepdims=True))
                alpha = jnp.exp2(m_prev - m_new)
                p = jnp.concatenate(
                    [jnp.exp2(s[:, c * LANES:(c + 1) * LANES] - m_new).astype(BF16)
                     for c in range(nk // LANES)], axis=1)
                pv = jnp.dot(p, v_ref[:nk, sl], preferred_element_type=F32)
                acc_scr[hh, rows] = alpha * acc_scr[hh, rows] + pv
                m_scr[hh, rows] = m_new

    @pl.when(ki < qi)
    def _():
        step(False)

    @pl.when(ki == qi)
    def _():
        step(True)
        lane = lax.broadcasted_iota(jnp.int32, (tq, LANES), 1)
        a0 = acc_scr[0]
        a1 = acc_scr[1]
        o0 = a0 / pltpu.roll(a0, MLA_V, 1)
        o1 = pltpu.roll(a1, MLA_V, 1) / a1
        o_ref[...] = jnp.where(lane < MLA_V, o0, o1).astype(o_ref.dtype)


def _attention(q, k, v, batch, seq, tq):
    nq = seq // tq
    qi_tab = np.array([qi for qi in range(nq) for _ in range(qi + 1)], np.int32)
    ki_tab = np.array([ki for qi in range(nq) for ki in range(qi + 1)], np.int32)
    npairs = len(qi_tab)
    hp = MLA_HEADS // 2
    blk = lambda tab: pl.BlockSpec((tq, 2 * LANES), lambda b, h, p, qt, kt: (b * nq + (qt, kt)[tab][p], h))
    grid_spec = pltpu.PrefetchScalarGridSpec(
        num_scalar_prefetch=2,
        grid=(batch, hp, npairs),
        in_specs=[blk(0), blk(1), blk(1)],
        out_specs=pl.BlockSpec((tq, 2 * MLA_V), lambda b, h, p, qt, kt: (b * nq + qt[p], h)),
        scratch_shapes=[pltpu.VMEM((2, tq, LANES), F32), pltpu.VMEM((2, tq, LANES), F32)],
    )
    return pl.pallas_call(
        _attn_kernel,
        grid_spec=grid_spec,
        out_shape=jax.ShapeDtypeStruct((batch * seq, MLA_HEADS * MLA_V), BF16),
        compiler_params=_params(("parallel", "parallel", "arbitrary")),
        name="mla_attention",
    )(jnp.asarray(qi_tab), jnp.asarray(ki_tab), q, k, v)


def _sort_network(n):
    pairs = []
    p = 1
    while p < n:
        k = p
        while k >= 1:
            for j in range(k % p, n - k, 2 * k):
                for i in range(min(k, n - j - k)):
                    if (i + j) // (2 * p) == (i + j + k) // (2 * p):
                        pairs.append((i + j, i + j + k))
            k //= 2
        p *= 2
    return pairs


def _sorted_top(cur, n, val_scr, idx_scr=None):
    sub = SUBLANES
    groups = cur.shape[0] // sub
    track = idx_scr is not None
    base = lax.broadcasted_iota(jnp.int32, (sub, LANES), 0).astype(F32)
    network = _sort_network(groups)
    for c in range(cur.shape[1] // LANES):
        csl = slice(c * LANES, (c + 1) * LANES)
        vals = [cur[g * sub:(g + 1) * sub, csl] for g in range(groups)]
        if track:
            idxs = [base + float(g * sub) for g in range(groups)]
        for a, b in network:
            if track:
                keep = vals[a] >= vals[b]
                idxs[a], idxs[b] = jnp.where(keep, idxs[a], idxs[b]), jnp.where(keep, idxs[b], idxs[a])
            vals[a], vals[b] = jnp.maximum(vals[a], vals[b]), jnp.minimum(vals[a], vals[b])
        for r in range(n):
            m = jnp.max(vals[0], axis=0, keepdims=True)
            val_scr[r:r + 1, csl] = m
            hit = vals[0] == m
            if track:
                first = jnp.min(jnp.where(hit, base, float(sub)), axis=0, keepdims=True)
                hit = base == first
                idx_scr[r:r + 1, csl] = jnp.max(jnp.where(hit, idxs[0], -1.0), axis=0, keepdims=True)
            depth = min(n - 1 - r, groups)
            for k in range(depth):
                below = vals[k + 1] if k + 1 < groups else NEG_INF
                vals[k] = jnp.where(hit, below, vals[k])
                if track and k + 1 < groups:
                    idxs[k] = jnp.where(hit, idxs[k + 1], idxs[k])


def _peer_route_kernel(res_ref, a_ref, wo_ref, g_ref, wq_ref, k0_ref, k1_ref,
                       h_ref, x_ref, idx_ref, a2_ref, th_ref, s1_ref, b2_ref,
                       a_scr, b_scr, c_scr, i_scr, idx_all, a2_all):
    tr = res_ref.shape[0]
    h = res_ref[...] + jnp.dot(a_ref[...], wo_ref[...], preferred_element_type=F32)
    h_ref[...] = h
    hn = _rms(h, g_ref[...])
    x_ref[...] = (hn * (1.0 / math.sqrt(2.0))).astype(x_ref.dtype)
    q = jnp.dot(_split3(hn), wq_ref[...], preferred_element_type=F32)
    q3 = [_split3(q[:, hd * LANES:(hd + 1) * LANES]) for hd in range(PEER_HEADS)]
    nk = PEER_TOPK + 1
    rows = lax.broadcasted_iota(jnp.int32, (SUBLANES, tr), 0)
    a_scr[...] = jnp.full(a_scr.shape, NEG_INF, F32)
    b_scr[...] = jnp.full(b_scr.shape, NEG_INF, F32)
    contract_last = (((1,), (1,)), ((), ()))
    log2e = math.log2(math.e)
    for hd in range(PEER_HEADS):
        s0 = lax.dot_general(k0_ref[...], q3[hd], contract_last, preferred_element_type=F32)
        s1 = lax.dot_general(k1_ref[...], q3[hd], contract_last, preferred_element_type=F32)
        _sorted_top(s0, nk, a_scr, i_scr)
        _sorted_top(s1, nk, b_scr)
        pieces = [b_scr[...] + a_scr[0:1, :]]
        r = 1
        while nk // (r + 1) >= 2:
            pieces.append(jnp.where(rows < nk // (r + 1), b_scr[0:SUBLANES, :] + a_scr[r:r + 1, :], NEG_INF))
            r += 1
        assert r == SUBLANES
        pieces.append(a_scr[SUBLANES:PEER_RANK_ROWS, :] + b_scr[0:1, :])
        cand = jnp.concatenate(pieces, axis=0)
        padded = jnp.concatenate([cand, jnp.full((PEER_NKEYS - cand.shape[0], tr), NEG_INF, F32)], axis=0)
        _sorted_top(padded, nk, c_scr)
        best = c_scr[0:1, :]
        tau = 0.5 * (c_scr[PEER_TOPK - 1:PEER_TOPK, :] + c_scr[PEER_TOPK:PEER_TOPK + 1, :])
        z = jnp.sum(jnp.where(cand >= tau, jnp.exp(cand - best), 0.0), axis=0, keepdims=True)
        a_max = a_scr[0:1, :]
        a_rel = (a_scr[0:PEER_TOPK, :] - a_max) * log2e
        s1_rel = (s1 - (best - a_max + jnp.log(z) + 0.5 * math.log(2.0))) * log2e
        thr = (tau - (best + jnp.log(z) + 0.5 * math.log(2.0))) * log2e
        slots = slice(hd * PEER_TOPK, (hd + 1) * PEER_TOPK)
        idx_all[slots, :] = i_scr[0:PEER_TOPK, :]
        a2_all[slots, :] = jnp.exp2(a_rel)
        th_ref[slots, :] = thr - a_rel
        s1t = s1_rel.T
        s1_ref[hd] = s1t
        b2_ref[hd] = jnp.exp2(s1t)
    idx_ref[...] = idx_all[...].T
    a2_ref[...] = a2_all[...].T


def _peer_route(res, a, w_o, g, wq, k0, k1, tr):
    n, d = res.shape
    da = a.shape[1]
    nh = PEER_HEADS
    ns = nh * PEER_TOPK
    row = lambda c: pl.BlockSpec((tr, c), lambda i: (i, 0))
    per_head = pl.BlockSpec((nh, tr, PEER_NKEYS), lambda i: (0, i, 0))
    return pl.pallas_call(
        _peer_route_kernel,
        grid=(n // tr,),
        in_specs=[row(d), row(da), _full(w_o.shape), _full((1, d)), _full(wq.shape), _full(k0.shape),
                  _full(k1.shape)],
        out_specs=[row(d), row(d), row(ns), row(ns), pl.BlockSpec((ns, tr), lambda i: (0, i)),
                   per_head, per_head],
        out_shape=[jax.ShapeDtypeStruct((n, d), F32), jax.ShapeDtypeStruct((n, d), BF16),
                   jax.ShapeDtypeStruct((n, ns), F32), jax.ShapeDtypeStruct((n, ns), F32),
                   jax.ShapeDtypeStruct((ns, n), F32),
                   jax.ShapeDtypeStruct((nh, n, PEER_NKEYS), F32),
                   jax.ShapeDtypeStruct((nh, n, PEER_NKEYS), F32)],
        scratch_shapes=[pltpu.VMEM((PEER_RANK_ROWS, tr), F32)] * 4 + [pltpu.VMEM((ns, tr), F32)] * 2,
        compiler_params=_params(("parallel",)),
        name="peer_route",
    )(res, a, w_o, g, wq, k0, k1)


def _peer_weights_kernel(idx_ref, a2_ref, th_ref, s1_ref, b2_ref, g_ref):
    tg = idx_ref.shape[0]
    ns = th_ref.shape[0]
    sub = lax.broadcasted_iota(jnp.int32, (PEER_NKEYS, ns), 0).astype(F32)

    def head_rows(ref, t):
        return jnp.concatenate([jnp.broadcast_to(ref[hd, t:t + 1, :], (PEER_TOPK, PEER_NKEYS))
                                for hd in range(PEER_HEADS)], axis=0)

    for t in range(tg):
        first = jnp.where(sub == idx_ref[t:t + 1, :], a2_ref[t:t + 1, :], 0.0).astype(BF16)
        thr = jnp.broadcast_to(th_ref[:, t:t + 1], (ns, PEER_NKEYS))
        second = jnp.where(head_rows(s1_ref, t) >= thr, head_rows(b2_ref, t), 0.0).astype(BF16)
        gt = jnp.dot(first, second, preferred_element_type=F32)
        for blk in range(PEER_NKEYS // PEER_STEP_KEYS):
            g_ref[blk, t] = gt[blk * PEER_STEP_KEYS:(blk + 1) * PEER_STEP_KEYS, :]


def _peer_weights(idx, a2, th, s1, b2, tg):
    n, ns = idx.shape
    nh = PEER_HEADS
    nb = PEER_NKEYS // PEER_STEP_KEYS
    row = pl.BlockSpec((tg, ns), lambda i: (i, 0))
    per_head = pl.BlockSpec((nh, tg, PEER_NKEYS), lambda i: (0, i, 0))
    return pl.pallas_call(
        _peer_weights_kernel,
        grid=(n // tg,),
        in_specs=[row, row, pl.BlockSpec((ns, tg), lambda i: (0, i)), per_head, per_head],
        out_specs=pl.BlockSpec((nb, tg, PEER_STEP_KEYS, PEER_NKEYS), lambda i: (0, i, 0, 0)),
        out_shape=jax.ShapeDtypeStruct((nb, n, PEER_STEP_KEYS, PEER_NKEYS), F32),
        compiler_params=_params(("parallel",)),
        name="peer_weights",
    )(idx, a2, th, s1, b2)


def _peer_dense_kernel(x_ref, g_ref, u_ref, v_ref, h_ref, o_ref, acc_scr):
    e = pl.program_id(1)
    ne = pl.num_programs(1)
    t = x_ref.shape[0]
    rows_per_step = PEER_STEP_KEYS

    @pl.when(e == 0)
    def _():
        acc_scr[...] = jnp.zeros(acc_scr.shape, F32)

    y = jnp.dot(x_ref[...], u_ref[...], preferred_element_type=F32)
    ps = []
    for ii in range(rows_per_step):
        yc = y[:, ii * PEER_NKEYS:(ii + 1) * PEER_NKEYS]
        gi = g_ref[pl.ds(ii, t, stride=rows_per_step), :]
        ps.append((gi * (yc * (1.0 + lax.erf(yc)))).astype(BF16))
    acc_scr[...] += jnp.dot(jnp.concatenate(ps, axis=1), v_ref[...], preferred_element_type=F32)

    @pl.when(e == ne - 1)
    def _():
        o_ref[...] = h_ref[...] + acc_scr[...]


def _peer_dense(x, g, u, v, layer, h, t, eb):
    n, d = x.shape
    ne = v.shape[1] // eb
    tok = lambda c: pl.BlockSpec((t, c), lambda i, e: (i, 0))
    return pl.pallas_call(
        _peer_dense_kernel,
        grid=(n // t, ne),
        in_specs=[tok(d),
                  pl.BlockSpec((None, t * PEER_STEP_KEYS, PEER_NKEYS), lambda i, e: (e, i, 0)),
                  pl.BlockSpec((None, d, eb), lambda i, e: (layer, 0, e)),
                  pl.BlockSpec((None, eb, d), lambda i, e: (layer, e, 0)),
                  pl.BlockSpec((t, d), lambda i, e: (i, 0), pipeline_mode=pl.Buffered(1))],
        out_specs=tok(d),
        out_shape=jax.ShapeDtypeStruct((n, d), F32),
        scratch_shapes=[pltpu.VMEM((t, d), F32)],
        compiler_params=_params(("parallel", "arbitrary")),
        name="peer_dense",
    )(x, g, u, v, h)


def _gla_kernel(x_ref, gin_ref, win_ref, wg2_ref, bg_ref, gon_ref, tri_ref, o_ref, state_scr):
    tc = x_ref.shape[0]
    tb = tri_ref.shape[0]
    c_len = GLA_CHUNK
    n_chunks = tb // c_len
    hk = GLA_HEADS * GLA_DK
    hv = GLA_HEADS * GLA_DV

    @pl.when(pl.program_id(1) == 0)
    def _():
        state_scr[...] = jnp.zeros(state_scr.shape, F32)

    hn = _rms(x_ref[...], gin_ref[...])
    proj = jnp.dot(hn.astype(BF16), win_ref[...], preferred_element_type=F32)
    g_low = proj[:, 2 * hk + 2 * hv:]
    z = jnp.dot(g_low, wg2_ref[...], precision=HIGHEST, preferred_element_type=F32) + bg_ref[...]
    log_a = (jnp.minimum(z, 0.0) - jnp.log1p(jnp.exp(-jnp.abs(z)))) * (1.0 / GLA_TAU)
    hi = log_a.astype(BF16)
    rest = log_a - hi.astype(F32)
    mid = rest.astype(BF16)
    lo = (rest - mid.astype(F32)).astype(BF16)
    pieces = jnp.concatenate([hi, mid, lo], axis=1)

    row_chunk = lax.broadcasted_iota(jnp.int32, (tb, GLA_DK), 0) // c_len
    col_chunk = lax.broadcasted_iota(jnp.int32, (GLA_DK, tb), 1) // c_len
    causal = tri_ref[...] > 0
    contract_last = (((1,), (1,)), ((), ()))
    for blk in range(tc // tb):
        rows = slice(blk * tb, (blk + 1) * tb)
        cum = jnp.dot(tri_ref[...], pieces[rows, :], preferred_element_type=F32)
        b = cum[:, :hk] + cum[:, hk:2 * hk] + cum[:, 2 * hk:]
        b_last = jnp.concatenate(
            [jnp.broadcast_to(b[(c + 1) * c_len - 1:(c + 1) * c_len, :], (c_len, hk)) for c in range(n_chunks)],
            axis=0)
        q_dec = (proj[rows, :hk] * (GLA_DK ** -0.5) * jnp.exp(b)).astype(BF16)
        k_inv = proj[rows, hk:2 * hk] * jnp.exp(-b)
        dec = jnp.exp(b_last)
        k_rem = k_inv * dec
        k_inv = k_inv.astype(BF16)
        for h in range(GLA_HEADS):
            ksl = slice(h * GLA_DK, (h + 1) * GLA_DK)
            vsl = slice(2 * hk + h * GLA_DV, 2 * hk + (h + 1) * GLA_DV)
            rsl = slice(2 * hk + hv + h * GLA_DV, 2 * hk + hv + (h + 1) * GLA_DV)
            qd = q_dec[:, ksl]
            vh = proj[rows, vsl].astype(BF16)
            attn = lax.dot_general(qd, k_inv[:, ksl], contract_last, preferred_element_type=F32)
            o = jnp.dot(jnp.where(causal, attn, 0.0).astype(BF16), vh, preferred_element_type=F32)
            k_t = k_rem[:, ksl].T.astype(BF16)
            kv = jnp.dot(jnp.concatenate([jnp.where(col_chunk == c, k_t, 0.0) for c in range(n_chunks)], axis=0),
                         vh, preferred_element_type=F32)
            state = state_scr[h]
            states = []
            for c in range(n_chunks):
                states.append(state.astype(BF16))
                dec_col = jnp.broadcast_to(dec[c * c_len:c * c_len + 1, ksl], (8, GLA_DK)).T[:, 0:1]
                state = state * dec_col + kv[c * GLA_DK:(c + 1) * GLA_DK, :]
            state_scr[h] = state
            o = o + jnp.dot(jnp.concatenate([jnp.where(row_chunk == c, qd, 0.0) for c in range(n_chunks)], axis=1),
                            jnp.concatenate(states, axis=0), preferred_element_type=F32)
            rr = proj[rows, rsl]
            o_ref[rows, h * GLA_DV:(h + 1) * GLA_DV] = (
                _rms(o, gon_ref[...]) * (rr * jax.nn.sigmoid(rr))).astype(o_ref.dtype)


def _gla(x2, g_in, w, g_on, batch, seq, tc):
    n, d = x2.shape
    hk = GLA_HEADS * GLA_DK
    hv = GLA_HEADS * GLA_DV
    ns = seq // tc
    tb = min(GLA_BLOCK, tc)
    idx = np.arange(tb)
    tri = ((idx[:, None] // GLA_CHUNK == idx[None, :] // GLA_CHUNK) & (idx[None, :] <= idx[:, None]))
    return pl.pallas_call(
        _gla_kernel,
        grid=(batch, ns),
        in_specs=[pl.BlockSpec((tc, d), lambda b, s: (b * ns + s, 0)), _full((1, d)), _full(w["win"].shape),
                  _full(w["wg2"].shape), _full((1, hk)), _full((1, GLA_DV)), _full((tb, tb))],
        out_specs=pl.BlockSpec((tc, hv), lambda b, s: (b * ns + s, 0)),
        out_shape=jax.ShapeDtypeStruct((n, hv), BF16),
        scratch_shapes=[pltpu.VMEM((GLA_HEADS, GLA_DK, GLA_DV), F32)],
        compiler_params=_params(("parallel", "arbitrary")),
        name="gla",
    )(x2, g_in, w["win"], w["wg2"], w["bg"], g_on, jnp.asarray(tri, BF16))


def _prep_mla(w_down, g_q_lat, w_uq, g_kv_lat, w_ukv, g_qn, g_kn):
    d = w_down.shape[0]
    lat = MLA_Q_RANK + MLA_KV_RANK
    z = lambda c: jnp.zeros((d, c), F32)
    wd = jnp.concatenate([w_down[:, :lat], z(MLA_NOPE), w_down[:, lat:], z(LANES - MLA_QK)], axis=1)
    pad_head = LANES - MLA_QK
    wuq = jnp.pad(w_uq.reshape(MLA_Q_RANK, MLA_HEADS, MLA_QK), ((0, 0), (0, 0), (0, pad_head)))
    wukv = w_ukv.reshape(MLA_KV_RANK, MLA_HEADS, MLA_NOPE + MLA_V)
    wuk = jnp.pad(wukv[:, :, :MLA_NOPE], ((0, 0), (0, 0), (0, LANES - MLA_NOPE)))
    wuv = jnp.pad(wukv[:, :, MLA_NOPE:], ((0, 0), (0, 0), (0, LANES - MLA_V)))
    half = MLA_ROPE // 2
    inv_freq = ROPE_THETA ** (-jnp.arange(half, dtype=F32) / half)
    zero = lambda c: jnp.zeros((c,), F32)
    vone = jnp.tile(jnp.concatenate([zero(MLA_V), jnp.ones((LANES - MLA_V,), F32)]), MLA_HEADS)
    freq = jnp.concatenate([zero(MLA_NOPE), inv_freq, inv_freq, zero(pad_head)])
    rotw = np.zeros((2 * LANES, 2 * LANES), np.float32)
    for lane in range(MLA_NOPE, MLA_NOPE + half):
        rotw[lane + half, lane] = -1.0
        rotw[lane, lane + half] = 1.0
    rotw[LANES:, LANES:] = 1.0
    scale = MLA_QK ** -0.5 * math.log2(math.e)
    return {
        "wd": wd.astype(BF16),
        "gql": g_q_lat.reshape(1, -1), "gkl": g_kv_lat.reshape(1, -1),
        "wuq": wuq.reshape(MLA_Q_RANK, MLA_HEADS * LANES).astype(BF16),
        "wuk": wuk.reshape(MLA_KV_RANK, MLA_HEADS * LANES).astype(BF16),
        "wuv": wuv.reshape(MLA_KV_RANK, MLA_HEADS * LANES).astype(BF16),
        "vone": vone.reshape(1, MLA_HEADS * LANES),
        "gqn": (jnp.pad(g_qn, (0, pad_head)) * scale).reshape(1, LANES),
        "gkn": jnp.pad(g_kn, (0, pad_head)).reshape(1, LANES),
        "freq": freq.reshape(1, LANES), "rotw": jnp.asarray(rotw, BF16),
    }


def _prep_gla(w_in, w_g2, b_g):
    pad = LANES - GLA_GATE_RANK
    return {
        "win": jnp.pad(w_in, ((0, 0), (0, pad))).astype(BF16),
        "wg2": jnp.pad(w_g2, ((0, pad), (0, 0))),
        "bg": b_g.reshape(1, -1),
    }


def _prep_peer(w_query, sub_keys):
    zk = jnp.zeros((PEER_NKEYS, PEER_HALF), F32)
    return {
        "wq": _split3_const(w_query, 0),
        "k0": _split3_const(jnp.concatenate([sub_keys[0], zk], axis=1), 1),
        "k1": _split3_const(jnp.concatenate([zk, sub_keys[1]], axis=1), 1),
    }


def _peer_layer(res, a, w_o, g, pw, u_all, v_all, layer, tiles):
    h, xn, idx, a2, th, s1, b2 = _peer_route(res, a, w_o, g.reshape(1, -1), pw["wq"], pw["k0"], pw["k1"],
                                             tiles["route"])
    gate = _peer_weights(idx, a2, th, s1, b2, tiles["weights"])
    gate = gate.reshape(gate.shape[0], -1, PEER_NKEYS)
    return _peer_dense(xn, gate, u_all, v_all, layer, h, tiles["dense_t"], PEER_STEP_KEYS * PEER_NKEYS)


def _tiles(seq):
    pick = lambda want: math.gcd(want, seq)
    return {"proj": pick(512), "attn": pick(2048), "route": pick(512), "weights": pick(256),
            "dense_t": pick(1024), "gla": pick(1024)}


def kernel(x, positions, attn_norm_g, ffn_norm_g, mla_w_down, mla_g_q_lat, mla_w_uq, mla_g_kv_lat,
           mla_w_ukv, mla_g_qn, mla_g_kn, mla_w_o, gla_w_in, gla_w_g2, gla_b_g, gla_g_on, gla_w_o,
           peer_w_query, peer_sub_keys, peer_u, peer_v):
    batch, seq, d = x.shape
    n = batch * seq
    depth = attn_norm_g.shape[0]
    tiles = _tiles(seq)
    h = x.reshape(n, d)
    pos = positions.reshape(n, 1).astype(F32)
    u_all = peer_u.astype(BF16).transpose(0, 2, 1)
    v_all = peer_v.astype(BF16)
    for i in range(depth):
        j = i // 2
        g_in = attn_norm_g[i].reshape(1, d)
        if i % 2 == 0:
            w = _prep_mla(mla_w_down[j], mla_g_q_lat[j], mla_w_uq[j], mla_g_kv_lat[j], mla_w_ukv[j],
                          mla_g_qn[j], mla_g_kn[j])
            q, k, v = _mla_proj(h, pos, g_in, w, tiles["proj"])
            a = _attention(q, k, v, batch, seq, tiles["attn"])
            w_o = mla_w_o[j].astype(BF16)
        else:
            w = _prep_gla(gla_w_in[j], gla_w_g2[j], gla_b_g[j])
            a = _gla(h, g_in, w, gla_g_on[j].reshape(1, -1), batch, seq, tiles["gla"])
            w_o = gla_w_o[j].astype(BF16)
        pw = _prep_peer(peer_w_query[i], peer_sub_keys[i])
        h = _peer_layer(h, a, w_o, ffn_norm_g[i], pw, u_all, v_all, i, tiles)
    return h.reshape(batch, seq, d)
```
